```python
import math
import jax, jax.numpy as jnp
from jax import lax
import numpy as np

D_MODEL = 1024
BATCH = 4
SEQ = 4096
DEPTH = 2

A_WIDTH = D_MODEL // 2
A_GROUPS = 4
A_GROUP_DIM = A_WIDTH // A_GROUPS
CHUNK = 128
B_WIDTH = D_MODEL // 2
POOL_WINDOWS = (2, 4, 8, 16)
B_GROUP_DIM = B_WIDTH // len(POOL_WINDOWS)
HEAD_DIM = 64
C_HEADS = (3 * D_MODEL) // (8 * HEAD_DIM)
C_WIDTH = C_HEADS * 2 * HEAD_DIM
D_PATTERNS = ((128, 1), (512, 4), (2048, 16))
D_HEADS_PER_GROUP = D_MODEL // (4 * HEAD_DIM)
D_HEADS = D_HEADS_PER_GROUP * len(D_PATTERNS)
D_QKV_WIDTH = D_HEADS * HEAD_DIM
D_OUT_WIDTH = D_HEADS_PER_GROUP * HEAD_DIM
Q_BLOCK = 128
ROPE_THETA = 500000.0
ROPE_DIM = HEAD_DIM // 4
D_FF = 2816
CONV_WIDTH = 3
EPS = 1e-6
MAX_POS_OFFSET = 1024

kernel_name = "hybrid_gmlp_pool_diffattn_dilated_block"


def rms_norm(x, g, eps=EPS):
    xf = x.astype(jnp.float32)
    y = xf * lax.rsqrt(jnp.mean(xf * xf, axis=-1, keepdims=True) + eps)
    return (y * g.astype(jnp.float32)).astype(x.dtype)


def rope_tables(positions):
    inv = ROPE_THETA ** (-jnp.arange(0, ROPE_DIM, 2, dtype=jnp.float32) / ROPE_DIM)
    ang = positions.astype(jnp.float32)[..., None] * inv
    return jnp.cos(ang), jnp.sin(ang)


def apply_partial_rope(x, cos, sin):
    half = ROPE_DIM // 2
    xr = x[..., :ROPE_DIM].astype(jnp.float32)
    x1, x2 = xr[..., :half], xr[..., half:]
    c, s = cos[:, :, None, :], sin[:, :, None, :]
    rot = jnp.concatenate([x1 * c - x2 * s, x2 * c + x1 * s], axis=-1).astype(x.dtype)
    return jnp.concatenate([rot, x[..., ROPE_DIM:]], axis=-1)


def chunked_gmlp(z, v_gain, w_s, b_s):
    Bn, S, _ = z.shape
    u, v = jnp.split(z, 2, axis=-1)
    v = rms_norm(v, v_gain)
    nc = S // CHUNK
    v = v.reshape(Bn, nc, CHUNK, A_GROUPS, A_GROUP_DIM)
    causal = jnp.tril(jnp.ones((CHUNK, CHUNK), dtype=bool))
    w = jnp.where(causal[None], w_s, 0).astype(v.dtype)
    mixed = jnp.einsum('gts,bnsgc->bntgc', w, v) + b_s.T[:, :, None]
    return u * mixed.reshape(Bn, S, A_WIDTH)


def multiscale_pool(xb, pool_w, pool_scale):
    Bn, S, _ = xb.shape
    G = len(POOL_WINDOWS)
    xg = xb.reshape(Bn, S, G, B_GROUP_DIM)
    csum = jnp.pad(jnp.cumsum(xg.astype(jnp.float32), axis=1), ((0, 0), (1, 0), (0, 0), (0, 0)))
    t = jnp.arange(S)
    pooled = []
    for g, w in enumerate(POOL_WINDOWS):
        cg = csum[:, :, g]
        lagged = jnp.pad(cg, ((0, 0), (w - 1, 0), (0, 0)))[:, :S]
        count = jnp.minimum(t + 1, w).astype(jnp.float32)
        pooled.append((cg[:, 1:] - lagged) / count[None, :, None])
    pooled = jnp.stack(pooled, axis=2).astype(xb.dtype) - xg
    y = jnp.einsum('bsgc,gcd->bsgd', pooled, pool_w)
    return y.reshape(Bn, S, B_WIDTH) * pool_scale


def even_mixer(h, w_in, v_gain, w_s, b_s, pool_w, pool_scale, w_out):
    p = h @ w_in
    ya = chunked_gmlp(jax.nn.gelu(p[..., :2 * A_WIDTH]), v_gain, w_s, b_s)
    yb = multiscale_pool(p[..., 2 * A_WIDTH:], pool_w, pool_scale)
    return jnp.concatenate([ya, yb], axis=-1) @ w_out


def diff_attention(q, k, v, cos, sin, lq1, lk1, lq2, lk2, subln_gain, lambda_init):
    Bn, S, _ = q.shape
    f32 = jnp.float32
    q = apply_partial_rope(q.reshape(Bn, S, 2 * C_HEADS, HEAD_DIM), cos, sin)
    k = apply_partial_rope(k.reshape(Bn, S, 2 * C_HEADS, HEAD_DIM), cos, sin)
    v = v.reshape(Bn, S, C_HEADS, 2 * HEAD_DIM)
    lam = (jnp.exp(jnp.sum(lq1.astype(f32) * lk1.astype(f32)))
           - jnp.exp(jnp.sum(lq2.astype(f32) * lk2.astype(f32))) + lambda_init)
    nb = S // Q_BLOCK
    q_blocks = q.reshape(Bn, nb, Q_BLOCK, 2 * C_HEADS, HEAD_DIM).transpose(1, 0, 2, 3, 4)
    k_pos = jnp.arange(S)
    scale = HEAD_DIM ** -0.5

    def attend(args):
        qb, blk = args
        s = jnp.einsum('bqhd,bkhd->bhqk', qb, k).astype(f32) * scale
        q_pos = blk * Q_BLOCK + jnp.arange(Q_BLOCK)
        s = jnp.where(k_pos[None, :] <= q_pos[:, None], s, -jnp.inf)
        p = jax.nn.softmax(s, axis=-1).reshape(Bn, C_HEADS, 2, Q_BLOCK, S)
        a = p[:, :, 0] - lam * p[:, :, 1]
        return jnp.einsum('bhqk,bkhd->bqhd', a.astype(v.dtype), v)

    o = lax.map(attend, (q_blocks, jnp.arange(nb)))
    o = o.transpose(1, 0, 2, 3, 4).reshape(Bn, S, C_HEADS, 2 * HEAD_DIM)
    o = rms_norm(o, subln_gain) * (1.0 - lambda_init)
    return o.reshape(Bn, S, C_WIDTH)


def strided_band_attention(q, k, v, window, dilation):
    Bn, S, H, dh = q.shape
    nk = window // dilation
    L = S // dilation
    nb = -(-L // nk)
    Lp = nb * nk

    def strided(t):
        return t.reshape(Bn, L, dilation, H, dh).transpose(0, 2, 1, 3, 4)

    def banded(t):
        t = jnp.pad(strided(t), ((0, 0), (0, 0), (nk, Lp - L), (0, 0), (0, 0)))
        prev = t[:, :, :Lp].reshape(Bn, dilation, nb, nk, H, dh)
        cur = t[:, :, nk:].reshape(Bn, dilation, nb, nk, H, dh)
        return jnp.concatenate([prev, cur], axis=3)

    qs = jnp.pad(strided(q), ((0, 0), (0, 0), (0, Lp - L), (0, 0), (0, 0))).reshape(Bn, dilation, nb, nk, H, dh)
    kb, vb = banded(k), banded(v)
    a = jnp.arange(nk)[:, None]
    m = jnp.arange(2 * nk)[None, :]
    dist = a + nk - m
    k_pos = (jnp.arange(nb)[:, None, None] - 1) * nk + m[None]
    mask = (dist >= 0) & (dist <= nk) & (k_pos >= 0)
    s = jnp.einsum('brnqhd,brnkhd->brnhqk', qs, kb).astype(jnp.float32) * (HEAD_DIM ** -0.5)
    s = jnp.where(mask[:, None], s, -jnp.inf)
    lse = jax.nn.logsumexp(s, axis=-1)
    p = jnp.exp(s - lse[..., None])
    o = jnp.einsum('brnhqk,brnkhd->brnqhd', p.astype(v.dtype), vb)
    o = o.reshape(Bn, dilation, Lp, H, dh)[:, :, :L].transpose(0, 2, 1, 3, 4).reshape(Bn, S, H, dh)
    lse = lse.transpose(0, 1, 2, 4, 3).reshape(Bn, dilation, Lp, H)[:, :, :L]
    lse = lse.transpose(0, 2, 1, 3).reshape(Bn, S, H)
    return o, lse


def dilated_attention(q, k, v, cos, sin):
    Bn, S, _ = q.shape
    q = apply_partial_rope(q.reshape(Bn, S, D_HEADS, HEAD_DIM), cos, sin)
    k = apply_partial_rope(k.reshape(Bn, S, D_HEADS, HEAD_DIM), cos, sin)
    v = v.reshape(Bn, S, D_HEADS, HEAD_DIM)
    outs, lses = [], []
    for g, (window, dilation) in enumerate(D_PATTERNS):
        hs = slice(g * D_HEADS_PER_GROUP, (g + 1) * D_HEADS_PER_GROUP)
        o, lse = strided_band_attention(q[:, :, hs], k[:, :, hs], v[:, :, hs], window, dilation)
        outs.append(o)
        lses.append(lse)
    wts = jax.nn.softmax(jnp.stack(lses, axis=0), axis=0)
    o = jnp.sum(wts[..., None] * jnp.stack(outs, axis=0).astype(jnp.float32), axis=0)
    return o.astype(q.dtype).reshape(Bn, S, D_OUT_WIDTH)


def odd_mixer(h, cos, sin, w_in, lq1, lk1, lq2, lk2, subln_gain, w_out, layer):
    splits = [int(i) for i in np.cumsum([C_WIDTH, C_WIDTH, C_WIDTH, D_QKV_WIDTH, D_QKV_WIDTH])]
    cq, ck, cv, dq, dk, dv = jnp.split(h @ w_in, splits, axis=-1)
    lambda_init = 0.8 - 0.6 * math.exp(-0.3 * layer)
    yc = diff_attention(cq, ck, cv, cos, sin, lq1, lk1, lq2, lk2, subln_gain, lambda_init)
    yd = dilated_attention(dq, dk, dv, cos, sin)
    return jnp.concatenate([yc, yd], axis=-1) @ w_out


def conv_ffn(h, w_up, conv_w, conv_b, w_down):
    a, g = jnp.split(h @ w_up, 2, axis=-1)
    a = lax.conv_general_dilated(a, conv_w, window_strides=(1,), padding=((CONV_WIDTH - 1, 0),),
                                 dimension_numbers=('NWC', 'WIO', 'NWC'), feature_group_count=D_FF) + conv_b
    return (jax.nn.gelu(a) * g) @ w_down


def setup_inputs(seed: int = 0) -> dict:
    key = jax.random.key(seed)
    keys = iter(jax.random.split(key, 32))
    n_even, n_odd = (DEPTH + 1) // 2, DEPTH // 2
    f32 = jnp.float32

    def normal(shape, scale):
        return jax.random.normal(next(keys), shape, f32) * scale

    def gain(shape):
        return 1.0 + normal(shape, 0.02)

    x = normal((BATCH, SEQ, D_MODEL), 1.0)
    positions = (jnp.arange(SEQ, dtype=jnp.int32)[None, :]
                 + jax.random.randint(next(keys), (BATCH, 1), 0, MAX_POS_OFFSET, dtype=jnp.int32))
    odd_in = 3 * C_WIDTH + 3 * D_QKV_WIDTH
    return {
        'x': x,
        'positions': positions,
        'norm_mix': gain((DEPTH, D_MODEL)),
        'norm_ffn': gain((DEPTH, D_MODEL)),
        'final_norm': gain((D_MODEL,)),
        'even_w_in': normal((n_even, D_MODEL, 2 * A_WIDTH + B_WIDTH), D_MODEL ** -0.5),
        'gmlp_v_gain': gain((n_even, A_WIDTH)),
        'gmlp_w_s': normal((n_even, A_GROUPS, CHUNK, CHUNK), CHUNK ** -0.5),
        'gmlp_b_s': gain((n_even, A_GROUPS, CHUNK)),
        'pool_w': normal((n_even, len(POOL_WINDOWS), B_GROUP_DIM, B_GROUP_DIM), B_GROUP_DIM ** -0.5),
        'pool_scale': gain((n_even, B_WIDTH)),
        'even_w_out': normal((n_even, A_WIDTH + B_WIDTH, D_MODEL), (A_WIDTH + B_WIDTH) ** -0.5),
        'odd_w_in': normal((n_odd, D_MODEL, odd_in), D_MODEL ** -0.5),
        'lambda_q1': normal((n_odd, HEAD_DIM), 0.1),
        'lambda_k1': normal((n_odd, HEAD_DIM), 0.1),
        'lambda_q2': normal((n_odd, HEAD_DIM), 0.1),
        'lambda_k2': normal((n_odd, HEAD_DIM), 0.1),
        'subln_gain': gain((n_odd, 2 * HEAD_DIM)),
        'odd_w_out': normal((n_odd, C_WIDTH + D_OUT_WIDTH, D_MODEL), (C_WIDTH + D_OUT_WIDTH) ** -0.5),
        'ffn_w_up': normal((DEPTH, D_MODEL, 2 * D_FF), D_MODEL ** -0.5),
        'ffn_conv_w': normal((DEPTH, CONV_WIDTH, 1, D_FF), CONV_WIDTH ** -0.5),
        'ffn_conv_b': normal((DEPTH, D_FF), 0.02),
        'ffn_w_down': normal((DEPTH, D_FF, D_MODEL), D_FF ** -0.5),
    }


def reference(x, positions, norm_mix, norm_ffn, final_norm, even_w_in, gmlp_v_gain, gmlp_w_s, gmlp_b_s,
              pool_w, pool_scale, even_w_out, odd_w_in, lambda_q1, lambda_k1, lambda_q2, lambda_k2,
              subln_gain, odd_w_out, ffn_w_up, ffn_conv_w, ffn_conv_b, ffn_w_down):
    cos, sin = rope_tables(positions)
    h = x
    for layer in range(DEPTH):
        i = layer // 2
        hn = rms_norm(h, norm_mix[layer])
        if layer % 2 == 0:
            mix = even_mixer(hn, even_w_in[i], gmlp_v_gain[i], gmlp_w_s[i], gmlp_b_s[i],
                             pool_w[i], pool_scale[i], even_w_out[i])
        else:
            mix = odd_mixer(hn, cos, sin, odd_w_in[i], lambda_q1[i], lambda_k1[i], lambda_q2[i],
                            lambda_k2[i], subln_gain[i], odd_w_out[i], layer)
        h = h + mix
        h = h + conv_ffn(rms_norm(h, norm_ffn[layer]), ffn_w_up[layer], ffn_conv_w[layer],
                         ffn_conv_b[layer], ffn_w_down[layer])
    return rms_norm(h, final_norm)
```

```python
import functools
import math

import jax
import jax.numpy as jnp
from jax import lax
from jax.experimental import pallas as pl
from jax.experimental.pallas import tpu as pltpu

F32 = jnp.float32
BF16 = jnp.bfloat16

D_MODEL = 1024
A_WIDTH = 512
A_GROUPS = 4
CHUNK = 128
B_WIDTH = 512
POOL_WINDOWS = (2, 4, 8, 16)
HEAD_DIM = 64
C_HEADS = 6
C_WIDTH = 768
D_PATTERNS = ((128, 1), (512, 4), (2048, 16))
D_GROUP_WIDTH = 256
ROPE_THETA = 500000.0
ROPE_DIM = 16
D_FF = 2816
EPS = 1e-6
LAMBDA_INIT = 0.8 - 0.6 * math.exp(-0.3 * 1)

LANES = 128
HALO = 16
TOKEN_TILE = 512
FF_CHUNK = 256
ATT_BLOCK = 256
DIL_BLOCK = 128
VMEM_LIMIT = 56 * 1024 * 1024
NEG_BIG = -1e30


def _rms(x, g):
    return x * lax.rsqrt(jnp.mean(x * x, axis=-1, keepdims=True) + EPS) * g


def _gelu(x):
    c = math.sqrt(2.0 / math.pi)
    return 0.5 * x * (1.0 + jnp.tanh(c * (x + 0.044715 * (x * x * x))))


def _mm(a, b):
    return jnp.dot(a, b, preferred_element_type=F32)


def _mm_nt(a, b):
    return lax.dot_general(a, b, (((1,), (1,)), ((), ())), preferred_element_type=F32)


def _const_spec(shape):
    nd = len(shape)
    return pl.BlockSpec(shape, lambda *_: (0,) * nd, pipeline_mode=pl.Buffered(1))


def _tile_specs(tm):
    main = pl.BlockSpec((1, tm, D_MODEL), lambda b, i: (b, i, 0))
    halo = pl.BlockSpec((1, HALO, D_MODEL), lambda b, i: (b, jnp.maximum(i * (tm // HALO) - 1, 0), 0))
    return main, halo


def _params(n_axes):
    return pltpu.CompilerParams(dimension_semantics=("parallel",) * n_axes, vmem_limit_bytes=VMEM_LIMIT)


def _even_kernel(h_ref, halo_ref, g_ref, win_ref, vg_ref, ws_ref, bs_ref, pw_ref, ps_ref, wout_ref,
                 o_ref, xs_ref, pb_ref, y_ref):
    tm = h_ref.shape[1]
    i = pl.program_id(1)
    g = g_ref[...]
    x = h_ref[0]
    xh = jnp.where(i == 0, 0.0, halo_ref[0])
    xs_ref[0:HALO, :] = _rms(xh, g).astype(BF16)
    xs_ref[HALO:, :] = _rms(x, g).astype(BF16)

    z = _gelu(_mm(xs_ref[HALO:, :], win_ref[:, :2 * A_WIDTH]))
    u = z[:, :A_WIDTH]
    v = _rms(z[:, A_WIDTH:], vg_ref[...]).astype(BF16)
    n_chunks = tm // CHUNK
    row = lax.broadcasted_iota(jnp.int32, (CHUNK, CHUNK), 0)
    col = lax.broadcasted_iota(jnp.int32, (CHUNK, CHUNK), 1)
    for grp in range(A_GROUPS):
        lanes = slice(grp * LANES, (grp + 1) * LANES)
        w = jnp.where(row >= col, ws_ref[grp], 0.0).astype(BF16)
        rhs = jnp.concatenate([v[c * CHUNK:(c + 1) * CHUNK, lanes] for c in range(n_chunks)], axis=1)
        mixed = _mm(w, rhs) + bs_ref[:, grp:grp + 1]
        for c in range(n_chunks):
            rows = slice(c * CHUNK, (c + 1) * CHUNK)
            y_ref[rows, lanes] = (u[rows, lanes] * mixed[:, c * CHUNK:(c + 1) * CHUNK]).astype(BF16)

    pb_ref[...] = _mm(xs_ref[...], win_ref[:, 2 * A_WIDTH:])
    pos = i * tm + lax.broadcasted_iota(jnp.int32, (tm, 1), 0)
    for grp, window in enumerate(POOL_WINDOWS):
        lanes = slice(grp * LANES, (grp + 1) * LANES)
        cur = pb_ref[HALO:, lanes]
        acc = cur
        for k in range(1, window):
            acc = acc + pb_ref[pl.ds(HALO - k, tm), lanes]
        inv_count = 1.0 / jnp.minimum(pos + 1, window).astype(F32)
        pooled = (acc * inv_count - cur).astype(BF16)
        yb = _mm(pooled, pw_ref[grp]) * ps_ref[:, lanes]
        y_ref[:, A_WIDTH + grp * LANES:A_WIDTH + (grp + 1) * LANES] = yb.astype(BF16)

    o_ref[0] = x + _mm(y_ref[...], wout_ref[...])


def _even_mixer(h, gain, w_in, v_gain, w_s, b_s, pool_w, pool_scale, w_out):
    bn, s, _ = h.shape
    tm = TOKEN_TILE
    main, halo = _tile_specs(tm)
    return pl.pallas_call(
        _even_kernel,
        grid=(bn, s // tm),
        in_specs=[main, halo, _const_spec((1, D_MODEL)), _const_spec(w_in.shape), _const_spec((1, A_WIDTH)),
                  _const_spec(w_s.shape), _const_spec(b_s.shape), _const_spec(pool_w.shape),
                  _const_spec((1, B_WIDTH)), _const_spec(w_out.shape)],
        out_specs=main,
        out_shape=jax.ShapeDtypeStruct(h.shape, F32),
        scratch_shapes=[pltpu.VMEM((tm + HALO, D_MODEL), BF16), pltpu.VMEM((tm + HALO, B_WIDTH), F32),
                        pltpu.VMEM((tm, A_WIDTH + B_WIDTH), BF16)],
        compiler_params=_params(2),
        name="even_mixer",
    )(h, h, gain, w_in, v_gain, w_s, b_s, pool_w, pool_scale, w_out)


def _ffn_kernel(h_ref, halo_ref, g_ref, wup_ref, cw_ref, cb_ref, wdown_ref, fg_ref, o_ref,
                xs_ref, a_ref, hg_ref, *, final_norm):
    tm = h_ref.shape[1]
    i = pl.program_id(1)
    g = g_ref[...]
    x = h_ref[0]
    xh = jnp.where(i == 0, 0.0, halo_ref[0])
    xs_ref[0:HALO, :] = _rms(xh, g).astype(BF16)
    xs_ref[HALO:, :] = _rms(x, g).astype(BF16)
    for c in range(D_FF // FF_CHUNK):
        cols = slice(c * FF_CHUNK, (c + 1) * FF_CHUNK)
        a_ref[...] = _mm(xs_ref[...], wup_ref[:, cols])
        gate = _mm(xs_ref[HALO:, :], wup_ref[:, D_FF + c * FF_CHUNK:D_FF + (c + 1) * FF_CHUNK])
        conv = (a_ref[pl.ds(HALO - 2, tm), :] * cw_ref[0:1, cols]
                + a_ref[pl.ds(HALO - 1, tm), :] * cw_ref[1:2, cols]
                + a_ref[pl.ds(HALO, tm), :] * cw_ref[2:3, cols]
                + cb_ref[:, cols])
        hg_ref[:, cols] = (_gelu(conv) * gate).astype(BF16)
    y = x + _mm(hg_ref[...], wdown_ref[...])
    if final_norm:
        y = _rms(y, fg_ref[...])
    o_ref[0] = y


def _conv_ffn(h, gain, w_up, conv_w, conv_b, w_down, final_gain, final_norm):
    bn, s, _ = h.shape
    tm = TOKEN_TILE
    main, halo = _tile_specs(tm)
    return pl.pallas_call(
        functools.partial(_ffn_kernel, final_norm=final_norm),
        grid=(bn, s // tm),
        in_specs=[main, halo, _const_spec((1, D_MODEL)), _const_spec(w_up.shape), _const_spec(conv_w.shape),
                  _const_spec(conv_b.shape), _const_spec(w_down.shape), _const_spec((1, D_MODEL))],
        out_specs=main,
        out_shape=jax.ShapeDtypeStruct(h.shape, F32),
        scratch_shapes=[pltpu.VMEM((tm + HALO, D_MODEL), BF16), pltpu.VMEM((tm + HALO, FF_CHUNK), F32),
                        pltpu.VMEM((tm, D_FF), BF16)],
        compiler_params=_params(2),
        name="conv_ffn",
    )(h, h, gain, w_up, conv_w, conv_b, w_down, final_gain)


def _rope(x, tab):
    c, s_up, s_lo = tab
    return x * c + pltpu.roll(x, ROPE_DIM // 2, 1) * s_up + pltpu.roll(x, LANES - ROPE_DIM // 2, 1) * s_lo


def _qkv_kernel(h_ref, pos_ref, g_ref, inv_ref, wn_ref, w4_ref, w16_ref,
                cq_ref, ck_ref, cv_ref, dq0_ref, dk0_ref, dv0_ref, dq1_ref, dk1_ref, dv1_ref,
                dq2_ref, dk2_ref, dv2_ref, hn_ref, xs_ref, tab_ref, ptab_ref):
    tm = h_ref.shape[1]
    hn = _rms(h_ref[0], g_ref[...])
    for j in range(D_MODEL // LANES):
        hn_ref[j] = hn[:, j * LANES:(j + 1) * LANES]
    xs_ref[...] = hn.astype(BF16)

    ang = pos_ref[0].astype(F32) * inv_ref[...]
    lane = lax.broadcasted_iota(jnp.int32, (tm, LANES), 1) % HEAD_DIM
    cos, sin = jnp.cos(ang), jnp.sin(ang)
    half = ROPE_DIM // 2
    tab_ref[0] = jnp.where(lane < ROPE_DIM, cos, 1.0)
    tab_ref[1] = jnp.where((lane >= half) & (lane < ROPE_DIM), sin, 0.0)
    tab_ref[2] = jnp.where(lane < half, -sin, 0.0)

    scale = HEAD_DIM ** -0.5

    def project(w_ref, col0, width, tabs, rope, scaled):
        outs = []
        for c in range(width // LANES):
            r = _mm(xs_ref[...], w_ref[:, col0 + c * LANES:col0 + (c + 1) * LANES])
            if rope:
                r = _rope(r, tabs)
            if scaled:
                r = r * scale
            outs.append(r.astype(BF16))
        return outs

    nat = (tab_ref[0], tab_ref[1], tab_ref[2])
    for c, r in enumerate(project(wn_ref, 0, C_WIDTH, nat, True, True)):
        cq_ref[0, :, c * LANES:(c + 1) * LANES] = r
    for c, r in enumerate(project(wn_ref, C_WIDTH, C_WIDTH, nat, True, False)):
        ck_ref[0, :, c * LANES:(c + 1) * LANES] = r
    for c, r in enumerate(project(wn_ref, 2 * C_WIDTH, C_WIDTH, nat, False, False)):
        cv_ref[0, :, c * LANES:(c + 1) * LANES] = r
    gw = D_GROUP_WIDTH
    for c, r in enumerate(project(wn_ref, 3 * C_WIDTH, gw, nat, True, True)):
        dq0_ref[0, 0, :, c * LANES:(c + 1) * LANES] = r
    for c, r in enumerate(project(wn_ref, 3 * C_WIDTH + gw, gw, nat, True, False)):
        dk0_ref[0, 0, :, c * LANES:(c + 1) * LANES] = r
    for c, r in enumerate(project(wn_ref, 3 * C_WIDTH + 2 * gw, gw, nat, False, False)):
        dv0_ref[0, 0, :, c * LANES:(c + 1) * LANES] = r

    for dil, w_ref, q_ref, k_ref, v_ref in ((4, w4_ref, dq1_ref, dk1_ref, dv1_ref),
                                            (16, w16_ref, dq2_ref, dk2_ref, dv2_ref)):
        n = tm // dil
        for r in range(dil):
            rows = slice(r * n, (r + 1) * n)
            for j in range(D_MODEL // LANES):
                xs_ref[rows, j * LANES:(j + 1) * LANES] = hn_ref[j, pl.ds(r, n, stride=dil), :].astype(BF16)
            for t in range(3):
                ptab_ref[t, rows, :] = tab_ref[t, pl.ds(r, n, stride=dil), :]
        perm = (ptab_ref[0], ptab_ref[1], ptab_ref[2])
        for ref, col0, rope, scaled in ((q_ref, 0, True, True), (k_ref, gw, True, False), (v_ref, 2 * gw, False, False)):
            for c, res in enumerate(project(w_ref, col0, gw, perm, rope, scaled)):
                for r in range(dil):
                    ref[0, r, :, c * LANES:(c + 1) * LANES] = res[r * n:(r + 1) * n]


def _qkv_proj(h, positions, gain, inv_lanes, w_nat, w_d4, w_d16):
    bn, s, _ = h.shape
    tm = TOKEN_TILE
    main = pl.BlockSpec((1, tm, D_MODEL), lambda b, i: (b, i, 0))
    pos_spec = pl.BlockSpec((1, tm, 1), lambda b, i: (b, i, 0))
    c_spec = pl.BlockSpec((1, tm, C_WIDTH), lambda b, i: (b, i, 0))
    out_shapes, out_specs = [jax.ShapeDtypeStruct((bn, s, C_WIDTH), BF16)] * 3, [c_spec] * 3
    for _, dil in D_PATTERNS:
        out_shapes += [jax.ShapeDtypeStruct((bn, dil, s // dil, D_GROUP_WIDTH), BF16)] * 3
        out_specs += [pl.BlockSpec((1, dil, tm // dil, D_GROUP_WIDTH), lambda b, i: (b, 0, i, 0))] * 3
    return pl.pallas_call(
        _qkv_kernel,
        grid=(bn, s // tm),
        in_specs=[main, pos_spec, _const_spec((1, D_MODEL)), _const_spec((1, LANES)), _const_spec(w_nat.shape),
                  _const_spec(w_d4.shape), _const_spec(w_d16.shape)],
        out_specs=out_specs,
        out_shape=out_shapes,
        scratch_shapes=[pltpu.VMEM((D_MODEL // LANES, tm, LANES), F32), pltpu.VMEM((tm, D_MODEL), BF16),
                        pltpu.VMEM((3, tm, LANES), F32), pltpu.VMEM((3, tm, LANES), F32)],
        compiler_params=_params(2),
        name="qkv_proj",
    )(h, positions.reshape(bn, s, 1), gain, inv_lanes, w_nat, w_d4, w_d16)


def _diff_kernel(q_ref, k_ref, v_ref, lam_ref, sg_ref, o_ref):
    tq = q_ref.shape[1]
    qi = pl.program_id(2)
    q = q_ref[0]
    lane = lax.broadcasted_iota(jnp.int32, (tq, LANES), 1)
    zero = jnp.zeros_like(q)
    q_halves = (jnp.where(lane < HEAD_DIM, q, zero), jnp.where(lane >= HEAD_DIM, q, zero))

    def step(kb, carry, masked):
        start = pl.multiple_of(kb * tq, tq)
        k = k_ref[0, pl.ds(start, tq), :]
        v = v_ref[0, pl.ds(start, tq), :]
        new = []
        for (m, l, acc), qh in zip(carry, q_halves):
            s = _mm_nt(qh, k)
            if masked:
                r = lax.broadcasted_iota(jnp.int32, (tq, tq), 0)
                c = lax.broadcasted_iota(jnp.int32, (tq, tq), 1)
                s = jnp.where(c <= r, s, NEG_BIG)
            m_new = jnp.maximum(m, jnp.max(s, axis=-1, keepdims=True))
            alpha = jnp.exp(m - m_new)
            e = jnp.exp(s - m_new)
            l_new = alpha * l + jnp.sum(e, axis=-1, keepdims=True)
            acc_new = alpha * acc + _mm(e.astype(BF16), v)
            new.append((m_new, l_new, acc_new))
        return tuple(new)

    init = tuple((jnp.full((tq, 1), NEG_BIG, F32), jnp.zeros((tq, 1), F32), jnp.zeros((tq, LANES), F32))
                 for _ in range(2))
    carry = lax.fori_loop(0, qi, lambda kb, c: step(kb, c, False), init)
    (_, l1, acc1), (_, l2, acc2) = step(qi, carry, True)

    lam_rows = lam_ref[...]
    lam = (jnp.exp(jnp.sum(lam_rows[0:1] * lam_rows[1:2], axis=-1, keepdims=True))
           - jnp.exp(jnp.sum(lam_rows[2:3] * lam_rows[3:4], axis=-1, keepdims=True)) + LAMBDA_INIT)
    o = acc1 / l1 - lam * (acc2 / l2)
    o_ref[0] = (_rms(o, sg_ref[...]) * (1.0 - LAMBDA_INIT)).astype(BF16)


def _diff_attention(cq, ck, cv, lam_rows, subln_gain):
    bn, s, _ = cq.shape
    tq = ATT_BLOCK
    return pl.pallas_call(
        _diff_kernel,
        grid=(bn, C_HEADS, s // tq),
        in_specs=[pl.BlockSpec((1, tq, LANES), lambda b, h, i: (b, i, h)),
                  pl.BlockSpec((1, s, LANES), lambda b, h, i: (b, 0, h)),
                  pl.BlockSpec((1, s, LANES), lambda b, h, i: (b, 0, h)),
                  pl.BlockSpec(lam_rows.shape, lambda b, h, i: (0, 0)),
                  pl.BlockSpec((1, LANES), lambda b, h, i: (0, 0))],
        out_specs=pl.BlockSpec((1, tq, LANES), lambda b, h, i: (b, i, h)),
        out_shape=jax.ShapeDtypeStruct((bn, s, C_WIDTH), BF16),
        compiler_params=_params(3),
        name="diff_attention",
    )(cq, ck, cv, lam_rows, subln_gain)


def _dil_kernel(q0, k0, v0, q1, k1, v1, q2, k2, v2, o_ref, os_ref, ls_ref):
    nk = DIL_BLOCK
    s_len = o_ref.shape[1]
    lane = lax.broadcasted_iota(jnp.int32, (nk, LANES), 1)
    first_head = lane < HEAD_DIM
    row = lax.broadcasted_iota(jnp.int32, (nk, 2 * nk), 0)
    col = lax.broadcasted_iota(jnp.int32, (nk, 2 * nk), 1)

    for grp, ((_, dil), q_ref, k_ref, v_ref) in enumerate(zip(D_PATTERNS, (q0, q1, q2), (k0, k1, k2), (v0, v1, v2))):
        nb = s_len // dil // nk

        def block(t, _, dil=dil, nb=nb, grp=grp, q_ref=q_ref, k_ref=k_ref, v_ref=v_ref):
            r = t // nb
            n = t % nb
            cur = pl.multiple_of(n * nk, nk)
            prev = pl.multiple_of(jnp.maximum(n - 1, 0) * nk, nk)
            q = q_ref[0, r, pl.ds(cur, nk), :]
            k = jnp.concatenate([k_ref[0, r, pl.ds(prev, nk), :], k_ref[0, r, pl.ds(cur, nk), :]], axis=0)
            v = jnp.concatenate([v_ref[0, r, pl.ds(prev, nk), :], v_ref[0, r, pl.ds(cur, nk), :]], axis=0)
            lo = jnp.where(n == 0, nk, 0)
            mask = ((col < nk) & (col >= row) & (col >= lo)) | ((col >= nk) & (col - nk <= row))
            zero = jnp.zeros_like(q)
            outs, lses = [], []
            for qh in (jnp.where(first_head, q, zero), jnp.where(first_head, zero, q)):
                s = jnp.where(mask, _mm_nt(qh, k), NEG_BIG)
                m = jnp.max(s, axis=-1, keepdims=True)
                e = jnp.exp(s - m)
                l = jnp.sum(e, axis=-1, keepdims=True)
                outs.append(_mm(e.astype(BF16), v) / l)
                lses.append(jnp.broadcast_to(m + jnp.log(l), (nk, LANES)))
            rows = pl.ds(n * (nk * dil) + r, nk, stride=dil)
            os_ref[grp, rows, :] = jnp.where(first_head, outs[0], outs[1])
            ls_ref[grp, rows, :] = jnp.where(first_head, lses[0], lses[1])
            return 0

        lax.fori_loop(0, dil * nb, block, 0)

    tile = 512
    for t in range(s_len // tile):
        rows = slice(t * tile, (t + 1) * tile)
        lse = [ls_ref[grp, rows, :] for grp in range(3)]
        top = jnp.maximum(jnp.maximum(lse[0], lse[1]), lse[2])
        wts = [jnp.exp(x - top) for x in lse]
        num = wts[0] * os_ref[0, rows, :] + wts[1] * os_ref[1, rows, :] + wts[2] * os_ref[2, rows, :]
        o_ref[0, rows, :] = (num / (wts[0] + wts[1] + wts[2])).astype(BF16)


def _dilated_attention(dqkv):
    bn = dqkv[0].shape[0]
    s = dqkv[0].shape[2]
    specs = []
    for (_, dil) in D_PATTERNS:
        specs += [pl.BlockSpec((1, dil, s // dil, LANES), lambda b, hp: (b, 0, 0, hp))] * 3
    return pl.pallas_call(
        _dil_kernel,
        grid=(bn, D_GROUP_WIDTH // LANES),
        in_specs=specs,
        out_specs=pl.BlockSpec((1, s, LANES), lambda b, hp: (b, 0, hp)),
        out_shape=jax.ShapeDtypeStruct((bn, s, D_GROUP_WIDTH), BF16),
        scratch_shapes=[pltpu.VMEM((3, s, LANES), F32), pltpu.VMEM((3, s, LANES), F32)],
        compiler_params=_params(2),
        name="dilated_attention",
    )(*dqkv)


def _out_kernel(h_ref, yc_ref, yd_ref, wc_ref, wd_ref, o_ref):
    o_ref[0] = h_ref[0] + _mm(yc_ref[0], wc_ref[...]) + _mm(yd_ref[0], wd_ref[...])


def _out_proj(h, yc, yd, w_c, w_d):
    bn, s, _ = h.shape
    tm = TOKEN_TILE
    main = pl.BlockSpec((1, tm, D_MODEL), lambda b, i: (b, i, 0))
    return pl.pallas_call(
        _out_kernel,
        grid=(bn, s // tm),
        in_specs=[main, pl.BlockSpec((1, tm, C_WIDTH), lambda b, i: (b, i, 0)),
                  pl.BlockSpec((1, tm, D_GROUP_WIDTH), lambda b, i: (b, i, 0)),
                  _const_spec(w_c.shape), _const_spec(w_d.shape)],
        out_specs=main,
        out_shape=jax.ShapeDtypeStruct(h.shape, F32),
        compiler_params=_params(2),
        name="odd_out_proj",
    )(h, yc, yd, w_c, w_d)


def kernel(x, positions, norm_mix, norm_ffn, final_norm, even_w_in, gmlp_v_gain, gmlp_w_s, gmlp_b_s, pool_w,
           pool_scale, even_w_out, odd_w_in, lambda_q1, lambda_k1, lambda_q2, lambda_k2, subln_gain, odd_w_out,
           ffn_w_up, ffn_conv_w, ffn_conv_b, ffn_w_down):
    row = lambda a: a.reshape(1, -1)
    bf = lambda a: a.astype(BF16)

    def ffn(h, layer, final):
        return _conv_ffn(h, row(norm_ffn[layer]), bf(ffn_w_up[layer]), ffn_conv_w[layer].reshape(3, D_FF),
                         row(ffn_conv_b[layer]), bf(ffn_w_down[layer]), row(final_norm), final)

    h = _even_mixer(x, row(norm_mix[0]), bf(even_w_in[0]), row(gmlp_v_gain[0]), gmlp_w_s[0], gmlp_b_s[0].T,
                    bf(pool_w[0]), row(pool_scale[0]), bf(even_w_out[0]))
    h = ffn(h, 0, False)

    w_in = bf(odd_w_in[0])
    dq0 = 3 * C_WIDTH
    dwidth = 3 * D_GROUP_WIDTH

    def group_cols(g):
        return jnp.concatenate([w_in[:, dq0 + t * dwidth + g * D_GROUP_WIDTH:dq0 + t * dwidth + (g + 1) * D_GROUP_WIDTH]
                                for t in range(3)], axis=1)

    w_nat = jnp.concatenate([w_in[:, :dq0], group_cols(0)], axis=1)
    inv = ROPE_THETA ** (-jnp.arange(0, ROPE_DIM, 2, dtype=F32) / ROPE_DIM)
    inv_lanes = jnp.tile(jnp.concatenate([inv, inv, jnp.zeros((HEAD_DIM - ROPE_DIM,), F32)]), LANES // HEAD_DIM)
    qkv = _qkv_proj(h, positions, row(norm_mix[1]), row(inv_lanes), w_nat, group_cols(1), group_cols(2))
    lam_rows = jnp.stack([lambda_q1[0], lambda_k1[0], lambda_q2[0], lambda_k2[0]])
    yc = _diff_attention(qkv[0], qkv[1], qkv[2], lam_rows, row(subln_gain[0]))
    yd = _dilated_attention(qkv[3:])
    w_out = bf(odd_w_out[0])
    h = _out_proj(h, yc, yd, w_out[:C_WIDTH], w_out[C_WIDTH:])
    return ffn(h, 1, True)
```

```python
import functools
import math

import jax
import jax.numpy as jnp
from jax import lax
from jax.experimental import pallas as pl
from jax.experimental.pallas import tpu as pltpu

F32 = jnp.float32
BF16 = jnp.bfloat16

D_MODEL = 1024
A_WIDTH = 512
A_GROUPS = 4
CHUNK = 128
B_WIDTH = 512
POOL_WINDOWS = (2, 4, 8, 16)
HEAD_DIM = 64
C_HEADS = 6
C_WIDTH = 768
D_PATTERNS = ((128, 1), (512, 4), (2048, 16))
D_GROUP_WIDTH = 256
ROPE_THETA = 500000.0
ROPE_DIM = 16
D_FF = 2816
EPS = 1e-6
LAMBDA_INIT = 0.8 - 0.6 * math.exp(-0.3 * 1)

LANES = 128
HALO = 16
TOKEN_TILE = 512
FF_CHUNK = 256
ATT_Q = 512
ATT_K = 256
MXU_COLS = 256
DIL_BATCH = 4
DIL_BLOCK = 128
VMEM_LIMIT = 56 * 1024 * 1024
NEG_BIG = -1e30


def _rms(x, g):
    return x * lax.rsqrt(jnp.mean(x * x, axis=-1, keepdims=True) + EPS) * g


def _gelu(x):
    c = math.sqrt(2.0 / math.pi)
    return 0.5 * x * (1.0 + jnp.tanh(c * (x + 0.044715 * (x * x * x))))


def _mm(a, b):
    return jnp.dot(a, b, preferred_element_type=F32)


def _mm_nt(a, b):
    return lax.dot_general(a, b, (((1,), (1,)), ((), ())), preferred_element_type=F32)


def _const_spec(shape):
    nd = len(shape)
    return pl.BlockSpec(shape, lambda *_: (0,) * nd, pipeline_mode=pl.Buffered(1))


def _tile_specs(tm):
    main = pl.BlockSpec((1, tm, D_MODEL), lambda b, i: (b, i, 0))
    halo = pl.BlockSpec((1, HALO, D_MODEL), lambda b, i: (b, jnp.maximum(i * (tm // HALO) - 1, 0), 0))
    return main, halo


def _params(n_axes):
    return pltpu.CompilerParams(dimension_semantics=("parallel",) * n_axes, vmem_limit_bytes=VMEM_LIMIT)


def _even_kernel(h_ref, halo_ref, g_ref, win_ref, vg_ref, ws_ref, bs_ref, pw_ref, ps_ref, wout_ref,
                 o_ref, xs_ref, pb_ref, y_ref):
    tm = h_ref.shape[1]
    i = pl.program_id(1)
    g = g_ref[...]
    x = h_ref[0]
    xh = jnp.where(i == 0, 0.0, halo_ref[0])
    xs_ref[0:HALO, :] = _rms(xh, g).astype(BF16)
    xs_ref[HALO:, :] = _rms(x, g).astype(BF16)

    z = _gelu(_mm(xs_ref[HALO:, :], win_ref[:, :2 * A_WIDTH]))
    u = z[:, :A_WIDTH]
    v = _rms(z[:, A_WIDTH:], vg_ref[...]).astype(BF16)
    n_chunks = tm // CHUNK
    row = lax.broadcasted_iota(jnp.int32, (CHUNK, CHUNK), 0)
    col = lax.broadcasted_iota(jnp.int32, (CHUNK, CHUNK), 1)
    for grp in range(A_GROUPS):
        lanes = slice(grp * LANES, (grp + 1) * LANES)
        w = jnp.where(row >= col, ws_ref[grp], 0.0).astype(BF16)
        rhs = jnp.concatenate([v[c * CHUNK:(c + 1) * CHUNK, lanes] for c in range(n_chunks)], axis=1)
        mixed = _mm(w, rhs) + bs_ref[:, grp:grp + 1]
        for c in range(n_chunks):
            rows = slice(c * CHUNK, (c + 1) * CHUNK)
            y_ref[rows, lanes] = (u[rows, lanes] * mixed[:, c * CHUNK:(c + 1) * CHUNK]).astype(BF16)

    pb_ref[...] = _mm(xs_ref[...], win_ref[:, 2 * A_WIDTH:])
    pos = i * tm + lax.broadcasted_iota(jnp.int32, (tm, 1), 0)
    for grp, window in enumerate(POOL_WINDOWS):
        lanes = slice(grp * LANES, (grp + 1) * LANES)
        cur = pb_ref[HALO:, lanes]
        acc = cur
        for k in range(1, window):
            acc = acc + pb_ref[pl.ds(HALO - k, tm), lanes]
        inv_count = 1.0 / jnp.minimum(pos + 1, window).astype(F32)
        pooled = (acc * inv_count - cur).astype(BF16)
        yb = _mm(pooled, pw_ref[grp]) * ps_ref[:, lanes]
        y_ref[:, A_WIDTH + grp * LANES:A_WIDTH + (grp + 1) * LANES] = yb.astype(BF16)

    o_ref[0] = x + _mm(y_ref[...], wout_ref[...])


def _even_mixer(h, gain, w_in, v_gain, w_s, b_s, pool_w, pool_scale, w_out):
    bn, s, _ = h.shape
    tm = TOKEN_TILE
    main, halo = _tile_specs(tm)
    return pl.pallas_call(
        _even_kernel,
        grid=(bn, s // tm),
        in_specs=[main, halo, _const_spec((1, D_MODEL)), _const_spec(w_in.shape), _const_spec((1, A_WIDTH)),
                  _const_spec(w_s.shape), _const_spec(b_s.shape), _const_spec(pool_w.shape),
                  _const_spec((1, B_WIDTH)), _const_spec(w_out.shape)],
        out_specs=main,
        out_shape=jax.ShapeDtypeStruct(h.shape, F32),
        scratch_shapes=[pltpu.VMEM((tm + HALO, D_MODEL), BF16), pltpu.VMEM((tm + HALO, B_WIDTH), F32),
                        pltpu.VMEM((tm, A_WIDTH + B_WIDTH), BF16)],
        compiler_params=_params(2),
        name="even_mixer",
    )(h, h, gain, w_in, v_gain, w_s, b_s, pool_w, pool_scale, w_out)


def _ffn_kernel(h_ref, halo_ref, g_ref, wup_ref, cw_ref, cb_ref, wdown_ref, fg_ref, o_ref,
                xs_ref, a_ref, hg_ref, *, final_norm):
    tm = h_ref.shape[1]
    i = pl.program_id(1)
    g = g_ref[...]
    x = h_ref[0]
    xh = jnp.where(i == 0, 0.0, halo_ref[0])
    xs_ref[0:HALO, :] = _rms(xh, g).astype(BF16)
    xs_ref[HALO:, :] = _rms(x, g).astype(BF16)
    for c in range(D_FF // FF_CHUNK):
        cols = slice(c * FF_CHUNK, (c + 1) * FF_CHUNK)
        a_ref[...] = _mm(xs_ref[...], wup_ref[:, cols])
        gate = _mm(xs_ref[HALO:, :], wup_ref[:, D_FF + c * FF_CHUNK:D_FF + (c + 1) * FF_CHUNK])
        conv = (a_ref[pl.ds(HALO - 2, tm), :] * cw_ref[0:1, cols]
                + a_ref[pl.ds(HALO - 1, tm), :] * cw_ref[1:2, cols]
                + a_ref[pl.ds(HALO, tm), :] * cw_ref[2:3, cols]
                + cb_ref[:, cols])
        hg_ref[:, cols] = (_gelu(conv) * gate).astype(BF16)
    y = x + _mm(hg_ref[...], wdown_ref[...])
    if final_norm:
        y = _rms(y, fg_ref[...])
    o_ref[0] = y


def _conv_ffn(h, gain, w_up, conv_w, conv_b, w_down, final_gain, final_norm):
    bn, s, _ = h.shape
    tm = TOKEN_TILE
    main, halo = _tile_specs(tm)
    return pl.pallas_call(
        functools.partial(_ffn_kernel, final_norm=final_norm),
        grid=(bn, s // tm),
        in_specs=[main, halo, _const_spec((1, D_MODEL)), _const_spec(w_up.shape), _const_spec(conv_w.shape),
                  _const_spec(conv_b.shape), _const_spec(w_down.shape), _const_spec((1, D_MODEL))],
        out_specs=main,
        out_shape=jax.ShapeDtypeStruct(h.shape, F32),
        scratch_shapes=[pltpu.VMEM((tm + HALO, D_MODEL), BF16), pltpu.VMEM((tm + HALO, FF_CHUNK), F32),
                        pltpu.VMEM((tm, D_FF), BF16)],
        compiler_params=_params(2),
        name="conv_ffn",
    )(h, h, gain, w_up, conv_w, conv_b, w_down, final_gain)


def _rope(x, tab):
    c, s_up, s_lo = tab
    return x * c + pltpu.roll(x, ROPE_DIM // 2, 1) * s_up + pltpu.roll(x, LANES - ROPE_DIM // 2, 1) * s_lo


def _qkv_kernel(h_ref, pos_ref, g_ref, inv_ref, wqt_ref, wvt_ref, wn_ref, w4_ref, w16_ref,
                cqt_ref, ck_ref, cvt_ref, dq0_ref, dk0_ref, dv0_ref, dq1_ref, dk1_ref, dv1_ref,
                dq2_ref, dk2_ref, dv2_ref, hn_ref, xs_ref, tab_ref, ptab_ref):
    tm = h_ref.shape[1]
    half = ROPE_DIM // 2
    hn = _rms(h_ref[0], g_ref[...])
    for j in range(D_MODEL // LANES):
        hn_ref[j] = hn[:, j * LANES:(j + 1) * LANES]
    xs_ref[...] = hn.astype(BF16)

    ang = inv_ref[...] * pos_ref[0].astype(F32)
    cos_t, sin_t = jnp.cos(ang), jnp.sin(ang)
    rest = HEAD_DIM - ROPE_DIM
    ones, z_half, z_rest = jnp.ones((rest, tm), F32), jnp.zeros((half, tm), F32), jnp.zeros((rest, tm), F32)
    tab_ref[0] = jnp.concatenate([cos_t, cos_t, ones] * 2, axis=0).T
    tab_ref[1] = jnp.concatenate([z_half, sin_t, z_rest] * 2, axis=0).T
    tab_ref[2] = jnp.concatenate([-sin_t, z_half, z_rest] * 2, axis=0).T

    qt = _mm_nt(wqt_ref[...], xs_ref[...])
    pieces = []
    for hd in range(C_WIDTH // HEAD_DIM):
        x1 = qt[hd * HEAD_DIM:hd * HEAD_DIM + half]
        x2 = qt[hd * HEAD_DIM + half:hd * HEAD_DIM + ROPE_DIM]
        pieces += [x1 * cos_t - x2 * sin_t, x2 * cos_t + x1 * sin_t, qt[hd * HEAD_DIM + ROPE_DIM:(hd + 1) * HEAD_DIM]]
    qt = jnp.concatenate(pieces, axis=0) * (HEAD_DIM ** -0.5 * math.log2(math.e))
    for hp in range(C_HEADS):
        cqt_ref[0, hp] = qt[hp * LANES:(hp + 1) * LANES].astype(BF16)

    vt = _mm_nt(wvt_ref[...], xs_ref[...])
    for hp in range(C_HEADS):
        for j in range(tm // ATT_K):
            cvt_ref[0, hp, j] = vt[hp * LANES:(hp + 1) * LANES, j * ATT_K:(j + 1) * ATT_K].astype(BF16)

    def project(w_ref, col0, width, tabs, rope, scale):
        outs = []
        for c in range(width // MXU_COLS):
            r = _mm(xs_ref[...], w_ref[:, col0 + c * MXU_COLS:col0 + (c + 1) * MXU_COLS])
            for x in (r[:, :LANES], r[:, LANES:]):
                if rope:
                    x = _rope(x, tabs)
                if scale is not None:
                    x = x * scale
                outs.append(x.astype(BF16))
        return outs

    dscale = HEAD_DIM ** -0.5
    nat = (tab_ref[0], tab_ref[1], tab_ref[2])
    for c, r in enumerate(project(wn_ref, 0, C_WIDTH, nat, True, None)):
        ck_ref[0, :, c * LANES:(c + 1) * LANES] = r
    gw = D_GROUP_WIDTH
    for c, r in enumerate(project(wn_ref, C_WIDTH, gw, nat, True, dscale)):
        dq0_ref[0, 0, :, c * LANES:(c + 1) * LANES] = r
    for c, r in enumerate(project(wn_ref, C_WIDTH + gw, gw, nat, True, None)):
        dk0_ref[0, 0, :, c * LANES:(c + 1) * LANES] = r
    for c, r in enumerate(project(wn_ref, C_WIDTH + 2 * gw, gw, nat, False, None)):
        dv0_ref[0, 0, :, c * LANES:(c + 1) * LANES] = r

    for dil, w_ref, q_ref, k_ref, v_ref in ((4, w4_ref, dq1_ref, dk1_ref, dv1_ref),
                                            (16, w16_ref, dq2_ref, dk2_ref, dv2_ref)):
        n = tm // dil
        for r in range(dil):
            rows = slice(r * n, (r + 1) * n)
            for j in range(D_MODEL // LANES):
                xs_ref[rows, j * LANES:(j + 1) * LANES] = hn_ref[j, pl.ds(r, n, stride=dil), :].astype(BF16)
            for t in range(3):
                ptab_ref[t, rows, :] = tab_ref[t, pl.ds(r, n, stride=dil), :]
        perm = (ptab_ref[0], ptab_ref[1], ptab_ref[2])
        for ref, col0, rope, scale in ((q_ref, 0, True, dscale), (k_ref, gw, True, None), (v_ref, 2 * gw, False, None)):
            for c, res in enumerate(project(w_ref, col0, gw, perm, rope, scale)):
                for r in range(dil):
                    ref[0, r, :, c * LANES:(c + 1) * LANES] = res[r * n:(r + 1) * n]


def _qkv_proj(h, positions, gain, inv_col, wq_t, wv_t, w_nat, w_d4, w_d16):
    bn, s, _ = h.shape
    tm = TOKEN_TILE
    main = pl.BlockSpec((1, tm, D_MODEL), lambda b, i: (b, i, 0))
    pos_spec = pl.BlockSpec((1, 1, tm), lambda b, i: (b, 0, i))
    out_shapes = [jax.ShapeDtypeStruct((bn, C_HEADS, LANES, s), BF16),
                  jax.ShapeDtypeStruct((bn, s, C_WIDTH), BF16),
                  jax.ShapeDtypeStruct((bn, C_HEADS, s // ATT_K, LANES, ATT_K), BF16)]
    out_specs = [pl.BlockSpec((1, C_HEADS, LANES, tm), lambda b, i: (b, 0, 0, i)),
                 pl.BlockSpec((1, tm, C_WIDTH), lambda b, i: (b, i, 0)),
                 pl.BlockSpec((1, C_HEADS, tm // ATT_K, LANES, ATT_K), lambda b, i: (b, 0, i, 0, 0))]
    for _, dil in D_PATTERNS:
        out_shapes += [jax.ShapeDtypeStruct((bn, dil, s // dil, D_GROUP_WIDTH), BF16)] * 3
        out_specs += [pl.BlockSpec((1, dil, tm // dil, D_GROUP_WIDTH), lambda b, i: (b, 0, i, 0))] * 3
    return pl.pallas_call(
        _qkv_kernel,
        grid=(bn, s // tm),
        in_specs=[main, pos_spec, _const_spec((1, D_MODEL)), _const_spec(inv_col.shape), _const_spec(wq_t.shape),
                  _const_spec(wv_t.shape), _const_spec(w_nat.shape), _const_spec(w_d4.shape),
                  _const_spec(w_d16.shape)],
        out_specs=out_specs,
        out_shape=out_shapes,
        scratch_shapes=[pltpu.VMEM((D_MODEL // LANES, tm, LANES), F32), pltpu.VMEM((tm, D_MODEL), BF16),
                        pltpu.VMEM((3, tm, LANES), F32), pltpu.VMEM((3, tm, LANES), F32)],
        compiler_params=_params(2),
        name="qkv_proj",
    )(h, positions.reshape(bn, 1, s), gain, inv_col, wq_t, wv_t, w_nat, w_d4, w_d16)


def _diff_kernel(qt_ref, k_ref, vt_ref, lam_ref, sg_ref, o_ref, qc_ref, s_ref, e_ref, m_ref, l_ref, acc_ref):
    tq, tk = ATT_Q, ATT_K
    qj = pl.program_id(2)
    qt = qt_ref[0, 0]
    row = lax.broadcasted_iota(jnp.int32, (LANES, tq), 0)
    zero = jnp.zeros_like(qt)
    qc_ref[:, :tq] = jnp.where(row < HEAD_DIM, qt, zero)
    qc_ref[:, tq:] = jnp.where(row >= HEAD_DIM, qt, zero)

    def qk(kb, buf):
        start = pl.multiple_of(kb * tk, tk)
        s_ref[buf] = _mm(k_ref[0, pl.ds(start, tk), :], qc_ref[...])

    def soft(buf, kb, masked):
        alphas = []
        for j in range(2 * tq // LANES):
            cols = slice(j * LANES, (j + 1) * LANES)
            s = s_ref[buf, :, cols]
            if masked:
                key = kb * tk + lax.broadcasted_iota(jnp.int32, (tk, LANES), 0)
                qry = qj * tq + (j * LANES) % tq + lax.broadcasted_iota(jnp.int32, (tk, LANES), 1)
                s = jnp.where(key <= qry, s, NEG_BIG)
            m_old = m_ref[:, cols]
            m_new = jnp.maximum(m_old, jnp.max(s, axis=0, keepdims=True))
            alpha = jnp.exp2(m_old - m_new)
            e = jnp.exp2(s - m_new)
            l_ref[:, cols] = alpha * l_ref[:, cols] + jnp.sum(e, axis=0, keepdims=True)
            m_ref[:, cols] = m_new
            e_ref[:, cols] = e.astype(BF16)
            alphas.append(alpha)
        pv = _mm(vt_ref[0, 0, kb], e_ref[...])
        acc_ref[...] = acc_ref[...] * jnp.concatenate(alphas, axis=1) + pv

    m_ref[...] = jnp.full(m_ref.shape, NEG_BIG, F32)
    l_ref[...] = jnp.zeros(l_ref.shape, F32)
    acc_ref[...] = jnp.zeros(acc_ref.shape, F32)
    qk(0, 0)

    def pair(it, _):
        qk(2 * it + 1, 1)
        soft(0, 2 * it, False)
        qk(2 * it + 2, 0)
        soft(1, 2 * it + 1, False)
        return 0

    lax.fori_loop(0, qj, pair, 0)
    qk(2 * qj + 1, 1)
    soft(0, 2 * qj, True)
    soft(1, 2 * qj + 1, True)

    lam_rows = lam_ref[...]
    lam = (jnp.exp(jnp.sum(lam_rows[0:1] * lam_rows[1:2], axis=-1, keepdims=True))
           - jnp.exp(jnp.sum(lam_rows[2:3] * lam_rows[3:4], axis=-1, keepdims=True)) + LAMBDA_INIT)
    inv = 1.0 / l_ref[...]
    ot = acc_ref[:, :tq] * inv[:, :tq] - lam * (acc_ref[:, tq:] * inv[:, tq:])
    ot = ot * lax.rsqrt(jnp.mean(ot * ot, axis=0, keepdims=True) + EPS) * (sg_ref[...] * (1.0 - LAMBDA_INIT))
    o_ref[0] = ot.T.astype(BF16)


def _diff_attention(cqt, ck, cvt, lam_rows, subln_gain_col):
    bn, s, _ = ck.shape
    tq, tk = ATT_Q, ATT_K
    return pl.pallas_call(
        _diff_kernel,
        grid=(bn, C_HEADS, s // tq),
        in_specs=[pl.BlockSpec((1, 1, LANES, tq), lambda b, h, i: (b, h, 0, i)),
                  pl.BlockSpec((1, s, LANES), lambda b, h, i: (b, 0, h)),
                  pl.BlockSpec((1, 1, s // tk, LANES, tk), lambda b, h, i: (b, h, 0, 0, 0)),
                  pl.BlockSpec(lam_rows.shape, lambda b, h, i: (0, 0)),
                  pl.BlockSpec((LANES, 1), lambda b, h, i: (0, 0))],
        out_specs=pl.BlockSpec((1, tq, LANES), lambda b, h, i: (b, i, h)),
        out_shape=jax.ShapeDtypeStruct((bn, s, C_WIDTH), BF16),
        scratch_shapes=[pltpu.VMEM((LANES, 2 * tq), BF16), pltpu.VMEM((2, tk, 2 * tq), F32),
                        pltpu.VMEM((tk, 2 * tq), BF16), pltpu.VMEM((1, 2 * tq), F32), pltpu.VMEM((1, 2 * tq), F32),
                        pltpu.VMEM((LANES, 2 * tq), F32)],
        compiler_params=_params(3),
        name="diff_attention",
    )(cqt, ck, cvt, lam_rows, subln_gain_col)


def _dil_kernel(q0, k0, v0, q1, k1, v1, q2, k2, v2, o_ref, os_ref, ls_ref):
    nk = DIL_BLOCK
    s_len = o_ref.shape[1]
    lane = lax.broadcasted_iota(jnp.int32, (nk, LANES), 1)
    first_head = lane < HEAD_DIM
    row = lax.broadcasted_iota(jnp.int32, (nk, 2 * nk), 0)
    col = lax.broadcasted_iota(jnp.int32, (nk, 2 * nk), 1)
    band = ((col < nk) & (col >= row)) | ((col >= nk) & (col - nk <= row))

    for grp, ((_, dil), q_ref, k_ref, v_ref) in enumerate(zip(D_PATTERNS, (q0, q1, q2), (k0, k1, k2), (v0, v1, v2))):
        nb = s_len // dil // nk
        per_iter = min(nb, DIL_BATCH)
        n_iter = nb // per_iter

        def batch(t, _, dil=dil, per_iter=per_iter, n_iter=n_iter, grp=grp, q_ref=q_ref, k_ref=k_ref, v_ref=v_ref):
            r = t // n_iter
            n0 = (t % n_iter) * per_iter
            base = pl.multiple_of(n0 * nk, nk)
            prev = pl.multiple_of(jnp.maximum(n0 - 1, 0) * nk, nk)
            q_all = q_ref[0, r, pl.ds(base, per_iter * nk), :]
            k_all = jnp.concatenate([k_ref[0, r, pl.ds(prev, nk), :], k_ref[0, r, pl.ds(base, per_iter * nk), :]], axis=0)
            v_all = jnp.concatenate([v_ref[0, r, pl.ds(prev, nk), :], v_ref[0, r, pl.ds(base, per_iter * nk), :]], axis=0)
            for i in range(per_iter):
                q = q_all[i * nk:(i + 1) * nk]
                k = k_all[i * nk:(i + 2) * nk]
                v = v_all[i * nk:(i + 2) * nk]
                mask = band
                if i == 0:
                    mask = band & (col >= jnp.where(n0 == 0, nk, 0))
                zero = jnp.zeros_like(q)
                outs, lses = [], []
                for qh in (jnp.where(first_head, q, zero), jnp.where(first_head, zero, q)):
                    s = jnp.where(mask, _mm_nt(qh, k), NEG_BIG)
                    m = jnp.max(s, axis=-1, keepdims=True)
                    e = jnp.exp(s - m)
                    l = jnp.sum(e, axis=-1, keepdims=True)
                    outs.append(_mm(e.astype(BF16), v) / l)
                    lses.append(jnp.broadcast_to(m + jnp.log(l), (nk, LANES)))
                rows = pl.ds((n0 + i) * (nk * dil) + r, nk, stride=dil)
                os_ref[grp, rows, :] = jnp.where(first_head, outs[0], outs[1])
                ls_ref[grp, rows, :] = jnp.where(first_head, lses[0], lses[1])
            return 0

        lax.fori_loop(0, dil * n_iter, batch, 0)

    tile = 512
    for t in range(s_len // tile):
        rows = slice(t * tile, (t + 1) * tile)
        lse = [ls_ref[grp, rows, :] for grp in range(3)]
        top = jnp.maximum(jnp.maximum(lse[0], lse[1]), lse[2])
        wts = [jnp.exp(x - top) for x in lse]
        num = wts[0] * os_ref[0, rows, :] + wts[1] * os_ref[1, rows, :] + wts[2] * os_ref[2, rows, :]
        o_ref[0, rows, :] = (num / (wts[0] + wts[1] + wts[2])).astype(BF16)


def _dilated_attention(dqkv):
    bn = dqkv[0].shape[0]
    s = dqkv[0].shape[2]
    specs = []
    for (_, dil) in D_PATTERNS:
        specs += [pl.BlockSpec((1, dil, s // dil, LANES), lambda b, hp: (b, 0, 0, hp))] * 3
    return pl.pallas_call(
        _dil_kernel,
        grid=(bn, D_GROUP_WIDTH // LANES),
        in_specs=specs,
        out_specs=pl.BlockSpec((1, s, LANES), lambda b, hp: (b, 0, hp)),
        out_shape=jax.ShapeDtypeStruct((bn, s, D_GROUP_WIDTH), BF16),
        scratch_shapes=[pltpu.VMEM((3, s, LANES), F32), pltpu.VMEM((3, s, LANES), F32)],
        compiler_params=_params(2),
        name="dilated_attention",
    )(*dqkv)


def _out_kernel(h_ref, yc_ref, yd_ref, wc_ref, wd_ref, o_ref):
    o_ref[0] = h_ref[0] + _mm(yc_ref[0], wc_ref[...]) + _mm(yd_ref[0], wd_ref[...])


def _out_proj(h, yc, yd, w_c, w_d):
    bn, s, _ = h.shape
    tm = TOKEN_TILE
    main = pl.BlockSpec((1, tm, D_MODEL), lambda b, i: (b, i, 0))
    return pl.pallas_call(
        _out_kernel,
        grid=(bn, s // tm),
        in_specs=[main, pl.BlockSpec((1, tm, C_WIDTH), lambda b, i: (b, i, 0)),
                  pl.BlockSpec((1, tm, D_GROUP_WIDTH), lambda b, i: (b, i, 0)),
                  _const_spec(w_c.shape), _const_spec(w_d.shape)],
        out_specs=main,
        out_shape=jax.ShapeDtypeStruct(h.shape, F32),
        compiler_params=_params(2),
        name="odd_out_proj",
    )(h, yc, yd, w_c, w_d)


def kernel(x, positions, norm_mix, norm_ffn, final_norm, even_w_in, gmlp_v_gain, gmlp_w_s, gmlp_b_s, pool_w,
           pool_scale, even_w_out, odd_w_in, lambda_q1, lambda_k1, lambda_q2, lambda_k2, subln_gain, odd_w_out,
           ffn_w_up, ffn_conv_w, ffn_conv_b, ffn_w_down):
    row = lambda a: a.reshape(1, -1)
    bf = lambda a: a.astype(BF16)

    def ffn(h, layer, final):
        return _conv_ffn(h, row(norm_ffn[layer]), bf(ffn_w_up[layer]), ffn_conv_w[layer].reshape(3, D_FF),
                         row(ffn_conv_b[layer]), bf(ffn_w_down[layer]), row(final_norm), final)

    h = _even_mixer(x, row(norm_mix[0]), bf(even_w_in[0]), row(gmlp_v_gain[0]), gmlp_w_s[0], gmlp_b_s[0].T,
                    bf(pool_w[0]), row(pool_scale[0]), bf(even_w_out[0]))
    h = ffn(h, 0, False)

    w_in = bf(odd_w_in[0])
    dq0 = 3 * C_WIDTH
    dwidth = 3 * D_GROUP_WIDTH

    def group_cols(g):
        return jnp.concatenate([w_in[:, dq0 + t * dwidth + g * D_GROUP_WIDTH:dq0 + t * dwidth + (g + 1) * D_GROUP_WIDTH]
                                for t in range(3)], axis=1)

    w_nat = jnp.concatenate([w_in[:, C_WIDTH:2 * C_WIDTH], group_cols(0)], axis=1)
    inv = ROPE_THETA ** (-jnp.arange(0, ROPE_DIM, 2, dtype=F32) / ROPE_DIM)
    qkv = _qkv_proj(h, positions, row(norm_mix[1]), inv.reshape(-1, 1), w_in[:, :C_WIDTH].T,
                    w_in[:, 2 * C_WIDTH:3 * C_WIDTH].T, w_nat, group_cols(1), group_cols(2))
    lam_rows = jnp.stack([lambda_q1[0], lambda_k1[0], lambda_q2[0], lambda_k2[0]])
    yc = _diff_attention(qkv[0], qkv[1], qkv[2], lam_rows, subln_gain[0].reshape(-1, 1))
    yd = _dilated_attention(qkv[3:])
    w_out = bf(odd_w_out[0])
    h = _out_proj(h, yc, yd, w_out[:C_WIDTH], w_out[C_WIDTH:])
    return ffn(h, 1, True)
```

```python
import functools
import math

import jax
import jax.numpy as jnp
from jax import lax
from jax.experimental import pallas as pl
from jax.experimental.pallas import tpu as pltpu

F32 = jnp.float32
BF16 = jnp.bfloat16

D_MODEL = 1024
A_WIDTH = 512
A_GROUPS = 4
CHUNK = 128
B_WIDTH = 512
POOL_WINDOWS = (2, 4, 8, 16)
HEAD_DIM = 64
C_HEADS = 6
C_WIDTH = 768
D_PATTERNS = ((128, 1), (512, 4), (2048, 16))
D_GROUP_WIDTH = 256
ROPE_THETA = 500000.0
ROPE_DIM = 16
D_FF = 2816
EPS = 1e-6
LAMBDA_INIT = 0.8 - 0.6 * math.exp(-0.3 * 1)

LANES = 128
HALO = 16
TOKEN_TILE = 512
FF_CHUNK = 256
ATT_Q = 512
ATT_K = 256
MXU_COLS = 256
V_ROWS = 144
DIL_BATCH = 8
DIL_BLOCK = 128
VMEM_LIMIT = 56 * 1024 * 1024
NEG_BIG = -1e30


def _rms(x, g):
    return x * lax.rsqrt(jnp.mean(x * x, axis=-1, keepdims=True) + EPS) * g


def _gelu(x):
    c = math.sqrt(2.0 / math.pi)
    return 0.5 * x * (1.0 + jnp.tanh(c * (x + 0.044715 * (x * x * x))))


def _mm(a, b):
    return jnp.dot(a, b, preferred_element_type=F32)


def _mm_nt(a, b):
    return lax.dot_general(a, b, (((1,), (1,)), ((), ())), preferred_element_type=F32)


def _const_spec(shape):
    nd = len(shape)
    return pl.BlockSpec(shape, lambda *_: (0,) * nd, pipeline_mode=pl.Buffered(1))


def _tile_specs(tm):
    main = pl.BlockSpec((1, tm, D_MODEL), lambda b, i: (b, i, 0))
    halo = pl.BlockSpec((1, HALO, D_MODEL), lambda b, i: (b, jnp.maximum(i * (tm // HALO) - 1, 0), 0))
    return main, halo


def _params(n_axes):
    return pltpu.CompilerParams(dimension_semantics=("parallel",) * n_axes, vmem_limit_bytes=VMEM_LIMIT)


def _even_kernel(h_ref, halo_ref, g_ref, win_ref, vg_ref, ws_ref, bs_ref, pw_ref, ps_ref, wout_ref,
                 o_ref, xs_ref, pb_ref, y_ref):
    tm = h_ref.shape[1]
    i = pl.program_id(1)
    g = g_ref[...]
    x = h_ref[0]
    xh = jnp.where(i == 0, 0.0, halo_ref[0])
    xs_ref[0:HALO, :] = _rms(xh, g).astype(BF16)
    xs_ref[HALO:, :] = _rms(x, g).astype(BF16)

    z = _gelu(_mm(xs_ref[HALO:, :], win_ref[:, :2 * A_WIDTH]))
    u = z[:, :A_WIDTH]
    v = _rms(z[:, A_WIDTH:], vg_ref[...]).astype(BF16)
    n_chunks = tm // CHUNK
    row = lax.broadcasted_iota(jnp.int32, (CHUNK, CHUNK), 0)
    col = lax.broadcasted_iota(jnp.int32, (CHUNK, CHUNK), 1)
    for grp in range(A_GROUPS):
        lanes = slice(grp * LANES, (grp + 1) * LANES)
        w = jnp.where(row >= col, ws_ref[grp], 0.0).astype(BF16)
        rhs = jnp.concatenate([v[c * CHUNK:(c + 1) * CHUNK, lanes] for c in range(n_chunks)], axis=1)
        mixed = _mm(w, rhs) + bs_ref[:, grp:grp + 1]
        for c in range(n_chunks):
            rows = slice(c * CHUNK, (c + 1) * CHUNK)
            y_ref[rows, lanes] = (u[rows, lanes] * mixed[:, c * CHUNK:(c + 1) * CHUNK]).astype(BF16)

    pb_ref[...] = _mm(xs_ref[...], win_ref[:, 2 * A_WIDTH:])
    pos = i * tm + lax.broadcasted_iota(jnp.int32, (tm, 1), 0)
    for grp, window in enumerate(POOL_WINDOWS):
        lanes = slice(grp * LANES, (grp + 1) * LANES)
        cur = pb_ref[HALO:, lanes]
        acc = cur
        for k in range(1, window):
            acc = acc + pb_ref[pl.ds(HALO - k, tm), lanes]
        inv_count = 1.0 / jnp.minimum(pos + 1, window).astype(F32)
        pooled = (acc * inv_count - cur).astype(BF16)
        yb = _mm(pooled, pw_ref[grp]) * ps_ref[:, lanes]
        y_ref[:, A_WIDTH + grp * LANES:A_WIDTH + (grp + 1) * LANES] = yb.astype(BF16)

    o_ref[0] = x + _mm(y_ref[...], wout_ref[...])


def _even_mixer(h, gain, w_in, v_gain, w_s, b_s, pool_w, pool_scale, w_out):
    bn, s, _ = h.shape
    tm = TOKEN_TILE
    main, halo = _tile_specs(tm)
    return pl.pallas_call(
        _even_kernel,
        grid=(bn, s // tm),
        in_specs=[main, halo, _const_spec((1, D_MODEL)), _const_spec(w_in.shape), _const_spec((1, A_WIDTH)),
                  _const_spec(w_s.shape), _const_spec(b_s.shape), _const_spec(pool_w.shape),
                  _const_spec((1, B_WIDTH)), _const_spec(w_out.shape)],
        out_specs=main,
        out_shape=jax.ShapeDtypeStruct(h.shape, F32),
        scratch_shapes=[pltpu.VMEM((tm + HALO, D_MODEL), BF16), pltpu.VMEM((tm + HALO, B_WIDTH), F32),
                        pltpu.VMEM((tm, A_WIDTH + B_WIDTH), BF16)],
        compiler_params=_params(2),
        name="even_mixer",
    )(h, h, gain, w_in, v_gain, w_s, b_s, pool_w, pool_scale, w_out)


def _ffn_kernel(h_ref, halo_ref, g_ref, wup_ref, cw_ref, cb_ref, wdown_ref, fg_ref, o_ref,
                xs_ref, a_ref, hg_ref, *, final_norm):
    tm = h_ref.shape[1]
    i = pl.program_id(1)
    g = g_ref[...]
    x = h_ref[0]
    xh = jnp.where(i == 0, 0.0, halo_ref[0])
    xs_ref[0:HALO, :] = _rms(xh, g).astype(BF16)
    xs_ref[HALO:, :] = _rms(x, g).astype(BF16)
    for c in range(D_FF // FF_CHUNK):
        cols = slice(c * FF_CHUNK, (c + 1) * FF_CHUNK)
        a_ref[...] = _mm(xs_ref[...], wup_ref[:, cols])
        gate = _mm(xs_ref[HALO:, :], wup_ref[:, D_FF + c * FF_CHUNK:D_FF + (c + 1) * FF_CHUNK])
        conv = (a_ref[pl.ds(HALO - 2, tm), :] * cw_ref[0:1, cols]
                + a_ref[pl.ds(HALO - 1, tm), :] * cw_ref[1:2, cols]
                + a_ref[pl.ds(HALO, tm), :] * cw_ref[2:3, cols]
                + cb_ref[:, cols])
        hg_ref[:, cols] = (_gelu(conv) * gate).astype(BF16)
    y = x + _mm(hg_ref[...], wdown_ref[...])
    if final_norm:
        y = _rms(y, fg_ref[...])
    o_ref[0] = y


def _conv_ffn(h, gain, w_up, conv_w, conv_b, w_down, final_gain, final_norm):
    bn, s, _ = h.shape
    tm = TOKEN_TILE
    main, halo = _tile_specs(tm)
    return pl.pallas_call(
        functools.partial(_ffn_kernel, final_norm=final_norm),
        grid=(bn, s // tm),
        in_specs=[main, halo, _const_spec((1, D_MODEL)), _const_spec(w_up.shape), _const_spec(conv_w.shape),
                  _const_spec(conv_b.shape), _const_spec(w_down.shape), _const_spec((1, D_MODEL))],
        out_specs=main,
        out_shape=jax.ShapeDtypeStruct(h.shape, F32),
        scratch_shapes=[pltpu.VMEM((tm + HALO, D_MODEL), BF16), pltpu.VMEM((tm + HALO, FF_CHUNK), F32),
                        pltpu.VMEM((tm, D_FF), BF16)],
        compiler_params=_params(2),
        name="conv_ffn",
    )(h, h, gain, w_up, conv_w, conv_b, w_down, final_gain)


def _rope(x, tab):
    c, s_up, s_lo = tab
    return x * c + pltpu.roll(x, ROPE_DIM // 2, 1) * s_up + pltpu.roll(x, LANES - ROPE_DIM // 2, 1) * s_lo


def _qkv_kernel(h_ref, pos_ref, g_ref, inv_ref, wqt_ref, wvt_ref, wn_ref, w4_ref, w16_ref,
                cqt_ref, ck_ref, cvt_ref, dq0_ref, dk0_ref, dv0_ref, dq1_ref, dk1_ref, dv1_ref,
                dq2_ref, dk2_ref, dv2_ref, hn_ref, xs_ref, tab_ref, ptab_ref):
    tm = h_ref.shape[1]
    half = ROPE_DIM // 2
    hn = _rms(h_ref[0], g_ref[...])
    for j in range(D_MODEL // LANES):
        hn_ref[j] = hn[:, j * LANES:(j + 1) * LANES]
    xs_ref[...] = hn.astype(BF16)

    ang = inv_ref[...] * pos_ref[0].astype(F32)
    cos_t, sin_t = jnp.cos(ang), jnp.sin(ang)
    rest = HEAD_DIM - ROPE_DIM
    ones, z_half, z_rest = jnp.ones((rest, tm), F32), jnp.zeros((half, tm), F32), jnp.zeros((rest, tm), F32)
    tab_ref[0] = jnp.concatenate([cos_t, cos_t, ones] * 2, axis=0).T
    tab_ref[1] = jnp.concatenate([z_half, sin_t, z_rest] * 2, axis=0).T
    tab_ref[2] = jnp.concatenate([-sin_t, z_half, z_rest] * 2, axis=0).T

    qt = _mm_nt(wqt_ref[...], xs_ref[...])
    pieces = []
    for hd in range(C_WIDTH // HEAD_DIM):
        x1 = qt[hd * HEAD_DIM:hd * HEAD_DIM + half]
        x2 = qt[hd * HEAD_DIM + half:hd * HEAD_DIM + ROPE_DIM]
        pieces += [x1 * cos_t - x2 * sin_t, x2 * cos_t + x1 * sin_t, qt[hd * HEAD_DIM + ROPE_DIM:(hd + 1) * HEAD_DIM]]
    qt = jnp.concatenate(pieces, axis=0) * (HEAD_DIM ** -0.5 * math.log2(math.e))
    for hp in range(C_HEADS):
        cqt_ref[0, hp] = qt[hp * LANES:(hp + 1) * LANES].astype(BF16)

    vt = _mm_nt(wvt_ref[...], xs_ref[...])
    for hp in range(C_HEADS):
        for j in range(tm // ATT_K):
            cvt_ref[0, hp, j] = vt[hp * LANES:(hp + 1) * LANES, j * ATT_K:(j + 1) * ATT_K].astype(BF16)

    def project(w_ref, col0, width, tabs, rope, scale):
        outs = []
        for c in range(width // MXU_COLS):
            r = _mm(xs_ref[...], w_ref[:, col0 + c * MXU_COLS:col0 + (c + 1) * MXU_COLS])
            for x in (r[:, :LANES], r[:, LANES:]):
                if rope:
                    x = _rope(x, tabs)
                if scale is not None:
                    x = x * scale
                outs.append(x.astype(BF16))
        return outs

    dscale = HEAD_DIM ** -0.5 * math.log2(math.e)
    nat = (tab_ref[0], tab_ref[1], tab_ref[2])
    for c, r in enumerate(project(wn_ref, 0, C_WIDTH, nat, True, None)):
        ck_ref[0, :, c * LANES:(c + 1) * LANES] = r
    gw = D_GROUP_WIDTH
    for c, r in enumerate(project(wn_ref, C_WIDTH, gw, nat, True, dscale)):
        dq0_ref[0, 0, :, c * LANES:(c + 1) * LANES] = r
    for c, r in enumerate(project(wn_ref, C_WIDTH + gw, gw, nat, True, None)):
        dk0_ref[0, 0, :, c * LANES:(c + 1) * LANES] = r
    for c, r in enumerate(project(wn_ref, C_WIDTH + 2 * gw, gw, nat, False, None)):
        dv0_ref[0, 0, :, c * LANES:(c + 1) * LANES] = r

    for dil, w_ref, q_ref, k_ref, v_ref in ((4, w4_ref, dq1_ref, dk1_ref, dv1_ref),
                                            (16, w16_ref, dq2_ref, dk2_ref, dv2_ref)):
        n = tm // dil
        for r in range(dil):
            rows = slice(r * n, (r + 1) * n)
            for j in range(D_MODEL // LANES):
                xs_ref[rows, j * LANES:(j + 1) * LANES] = hn_ref[j, pl.ds(r, n, stride=dil), :].astype(BF16)
            for t in range(3):
                ptab_ref[t, rows, :] = tab_ref[t, pl.ds(r, n, stride=dil), :]
        perm = (ptab_ref[0], ptab_ref[1], ptab_ref[2])
        for ref, col0, rope, scale in ((q_ref, 0, True, dscale), (k_ref, gw, True, None), (v_ref, 2 * gw, False, None)):
            for c, res in enumerate(project(w_ref, col0, gw, perm, rope, scale)):
                for r in range(dil):
                    ref[0, r, :, c * LANES:(c + 1) * LANES] = res[r * n:(r + 1) * n]


def _qkv_proj(h, positions, gain, inv_col, wq_t, wv_t, w_nat, w_d4, w_d16):
    bn, s, _ = h.shape
    tm = TOKEN_TILE
    main = pl.BlockSpec((1, tm, D_MODEL), lambda b, i: (b, i, 0))
    pos_spec = pl.BlockSpec((1, 1, tm), lambda b, i: (b, 0, i))
    out_shapes = [jax.ShapeDtypeStruct((bn, C_HEADS, LANES, s), BF16),
                  jax.ShapeDtypeStruct((bn, s, C_WIDTH), BF16),
                  jax.ShapeDtypeStruct((bn, C_HEADS, s // ATT_K, LANES, ATT_K), BF16)]
    out_specs = [pl.BlockSpec((1, C_HEADS, LANES, tm), lambda b, i: (b, 0, 0, i)),
                 pl.BlockSpec((1, tm, C_WIDTH), lambda b, i: (b, i, 0)),
                 pl.BlockSpec((1, C_HEADS, tm // ATT_K, LANES, ATT_K), lambda b, i: (b, 0, i, 0, 0))]
    for _, dil in D_PATTERNS:
        out_shapes += [jax.ShapeDtypeStruct((bn, dil, s // dil, D_GROUP_WIDTH), BF16)] * 3
        out_specs += [pl.BlockSpec((1, dil, tm // dil, D_GROUP_WIDTH), lambda b, i: (b, 0, i, 0))] * 3
    return pl.pallas_call(
        _qkv_kernel,
        grid=(bn, s // tm),
        in_specs=[main, pos_spec, _const_spec((1, D_MODEL)), _const_spec(inv_col.shape), _const_spec(wq_t.shape),
                  _const_spec(wv_t.shape), _const_spec(w_nat.shape), _const_spec(w_d4.shape),
                  _const_spec(w_d16.shape)],
        out_specs=out_specs,
        out_shape=out_shapes,
        scratch_shapes=[pltpu.VMEM((D_MODEL // LANES, tm, LANES), F32), pltpu.VMEM((tm, D_MODEL), BF16),
                        pltpu.VMEM((3, tm, LANES), F32), pltpu.VMEM((3, tm, LANES), F32)],
        compiler_params=_params(2),
        name="qkv_proj",
    )(h, positions.reshape(bn, 1, s), gain, inv_col, wq_t, wv_t, w_nat, w_d4, w_d16)


def _diff_kernel(qt_ref, k_ref, vt_ref, lam_ref, sg_ref, o_ref, qc_ref, s_ref, e_ref, m_ref, acc_ref):
    tq, tk = ATT_Q, ATT_K
    qj = pl.program_id(2)
    qt = qt_ref[0, 0]
    row = lax.broadcasted_iota(jnp.int32, (LANES, tq), 0)
    zero = jnp.zeros_like(qt)
    qc_ref[:, :tq] = jnp.where(row < HEAD_DIM, qt, zero)
    qc_ref[:, tq:] = jnp.where(row >= HEAD_DIM, qt, zero)
    ones = jnp.ones((V_ROWS - LANES, tk), BF16)

    n_strips = 2 * tq // LANES
    all_strips = tuple(range(n_strips))
    late = tuple(j for j in all_strips if (j * LANES) % tq >= tk)

    def qk(kb, buf, strips=all_strips):
        start = pl.multiple_of(kb * tk, tk)
        rhs = qc_ref[...] if strips == all_strips else jnp.concatenate(
            [qc_ref[:, j * LANES:(j + 1) * LANES] for j in strips], axis=1)
        s = _mm(k_ref[0, pl.ds(start, tk), :], rhs)
        for idx, j in enumerate(strips):
            s_ref[buf, j] = s[:, idx * LANES:(idx + 1) * LANES]

    def soft(buf, kb, strips=all_strips, key_off=None):
        alphas = []
        for j in strips:
            cols = slice(j * LANES, (j + 1) * LANES)
            s = s_ref[buf, j]
            q_rel = (j * LANES) % tq
            if key_off is not None and q_rel < key_off + tk:
                key = key_off + lax.broadcasted_iota(jnp.int32, (tk, LANES), 0)
                qry = q_rel + lax.broadcasted_iota(jnp.int32, (tk, LANES), 1)
                s = jnp.where(key <= qry, s, NEG_BIG)
            m_old = m_ref[:, cols]
            m_new = jnp.maximum(m_old, jnp.max(s, axis=0, keepdims=True))
            alphas.append(jnp.exp2(m_old - m_new))
            m_ref[:, cols] = m_new
            e_ref[j] = jnp.exp2(s - m_new).astype(BF16)
        lhs = jnp.concatenate([vt_ref[0, 0, kb], ones], axis=0)
        pv = _mm(lhs, jnp.concatenate([e_ref[j] for j in strips], axis=1))
        if strips == all_strips:
            acc_ref[...] = acc_ref[...] * jnp.concatenate(alphas, axis=1) + pv
        else:
            for idx, j in enumerate(strips):
                cols = slice(j * LANES, (j + 1) * LANES)
                acc_ref[:, cols] = acc_ref[:, cols] * alphas[idx] + pv[:, idx * LANES:(idx + 1) * LANES]

    m_ref[...] = jnp.full(m_ref.shape, NEG_BIG, F32)
    acc_ref[...] = jnp.zeros(acc_ref.shape, F32)
    qk(0, 0)

    def pair(it, _):
        qk(2 * it + 1, 1)
        soft(0, 2 * it)
        qk(2 * it + 2, 0)
        soft(1, 2 * it + 1)
        return 0

    lax.fori_loop(0, qj, pair, 0)
    qk(2 * qj + 1, 1, late)
    soft(0, 2 * qj, key_off=0)
    soft(1, 2 * qj + 1, late, key_off=tk)

    lam_rows = lam_ref[...]
    lam = (jnp.exp(jnp.sum(lam_rows[0:1] * lam_rows[1:2], axis=-1, keepdims=True))
           - jnp.exp(jnp.sum(lam_rows[2:3] * lam_rows[3:4], axis=-1, keepdims=True)) + LAMBDA_INIT)
    inv = 1.0 / acc_ref[LANES:LANES + 1, :]
    ot = acc_ref[0:LANES, :tq] * inv[:, :tq] - lam * (acc_ref[0:LANES, tq:] * inv[:, tq:])
    ot = ot * lax.rsqrt(jnp.mean(ot * ot, axis=0, keepdims=True) + EPS) * (sg_ref[...] * (1.0 - LAMBDA_INIT))
    o_ref[0] = ot.T.astype(BF16)


def _diff_attention(cqt, ck, cvt, lam_rows, subln_gain_col):
    bn, s, _ = ck.shape
    tq, tk = ATT_Q, ATT_K
    return pl.pallas_call(
        _diff_kernel,
        grid=(bn, C_HEADS, s // tq),
        in_specs=[pl.BlockSpec((1, 1, LANES, tq), lambda b, h, i: (b, h, 0, i)),
                  pl.BlockSpec((1, s, LANES), lambda b, h, i: (b, 0, h)),
                  pl.BlockSpec((1, 1, s // tk, LANES, tk), lambda b, h, i: (b, h, 0, 0, 0)),
                  pl.BlockSpec(lam_rows.shape, lambda b, h, i: (0, 0)),
                  pl.BlockSpec((LANES, 1), lambda b, h, i: (0, 0))],
        out_specs=pl.BlockSpec((1, tq, LANES), lambda b, h, i: (b, i, h)),
        out_shape=jax.ShapeDtypeStruct((bn, s, C_WIDTH), BF16),
        scratch_shapes=[pltpu.VMEM((LANES, 2 * tq), BF16), pltpu.VMEM((2, 2 * tq // LANES, tk, LANES), F32),
                        pltpu.VMEM((2 * tq // LANES, tk, LANES), BF16), pltpu.VMEM((1, 2 * tq), F32),
                        pltpu.VMEM((V_ROWS, 2 * tq), F32)],
        compiler_params=_params(3),
        name="diff_attention",
    )(cqt, ck, cvt, lam_rows, subln_gain_col)


def _dil_kernel(q0, k0, v0, q1, k1, v1, q2, k2, v2, o_ref, os_ref, ls_ref):
    nk = DIL_BLOCK
    s_len = o_ref.shape[1]
    lane = lax.broadcasted_iota(jnp.int32, (nk, LANES), 1)
    first_head = lane < HEAD_DIM
    row = lax.broadcasted_iota(jnp.int32, (2 * nk, 2 * nk), 0) % nk
    col = lax.broadcasted_iota(jnp.int32, (2 * nk, 2 * nk), 1)
    band = ((col < nk) & (col >= row)) | ((col >= nk) & (col - nk <= row))
    ones = jnp.ones((2 * nk, LANES), BF16)

    for grp, ((_, dil), q_ref, k_ref, v_ref) in enumerate(zip(D_PATTERNS, (q0, q1, q2), (k0, k1, k2), (v0, v1, v2))):
        nb = s_len // dil // nk
        per_res = min(nb, DIL_BATCH)
        n_res = DIL_BATCH // per_res
        chunks = nb // per_res

        def batch(t, _, dil=dil, per_res=per_res, n_res=n_res, chunks=chunks, grp=grp,
                  q_ref=q_ref, k_ref=k_ref, v_ref=v_ref):
            n0 = (t % chunks) * per_res
            base = pl.multiple_of(n0 * nk, nk)
            prev = pl.multiple_of(jnp.maximum(n0 - 1, 0) * nk, nk)
            for rr in range(n_res):
                r = (t // chunks) * n_res + rr
                q_all = q_ref[0, r, pl.ds(base, per_res * nk), :]
                k_all = jnp.concatenate([k_ref[0, r, pl.ds(prev, nk), :], k_ref[0, r, pl.ds(base, per_res * nk), :]], axis=0)
                v_all = jnp.concatenate([v_ref[0, r, pl.ds(prev, nk), :], v_ref[0, r, pl.ds(base, per_res * nk), :]], axis=0)
                for i in range(per_res):
                    q = q_all[i * nk:(i + 1) * nk]
                    k = k_all[i * nk:(i + 2) * nk]
                    v = jnp.concatenate([v_all[i * nk:(i + 2) * nk], ones], axis=1)
                    mask = band
                    if i == 0:
                        mask = band & (col >= jnp.where(n0 == 0, nk, 0))
                    zero = jnp.zeros_like(q)
                    q2 = jnp.concatenate([jnp.where(first_head, q, zero), jnp.where(first_head, zero, q)], axis=0)
                    s = jnp.where(mask, _mm_nt(q2, k), NEG_BIG)
                    m = jnp.max(s, axis=-1, keepdims=True)
                    pv = _mm(jnp.exp2(s - m).astype(BF16), v)
                    num = jnp.where(first_head, pv[:nk, :LANES], pv[nk:, :LANES])
                    den = jnp.where(first_head, pv[:nk, LANES:], pv[nk:, LANES:])
                    top = jnp.where(first_head, jnp.broadcast_to(m[:nk], (nk, LANES)), jnp.broadcast_to(m[nk:], (nk, LANES)))
                    rows = pl.ds((n0 + i) * (nk * dil) + r, nk, stride=dil)
                    os_ref[grp, rows, :] = num / den
                    ls_ref[grp, rows, :] = top + jnp.log2(den)
            return 0

        lax.fori_loop(0, (dil // n_res) * chunks, batch, 0)

    tile = 512
    for t in range(s_len // tile):
        rows = slice(t * tile, (t + 1) * tile)
        lse = [ls_ref[grp, rows, :] for grp in range(3)]
        top = jnp.maximum(jnp.maximum(lse[0], lse[1]), lse[2])
        wts = [jnp.exp2(x - top) for x in lse]
        num = wts[0] * os_ref[0, rows, :] + wts[1] * os_ref[1, rows, :] + wts[2] * os_ref[2, rows, :]
        o_ref[0, rows, :] = (num / (wts[0] + wts[1] + wts[2])).astype(BF16)


def _dilated_attention(dqkv):
    bn = dqkv[0].shape[0]
    s = dqkv[0].shape[2]
    specs = []
    for (_, dil) in D_PATTERNS:
        specs += [pl.BlockSpec((1, dil, s // dil, LANES), lambda b, hp: (b, 0, 0, hp))] * 3
    return pl.pallas_call(
        _dil_kernel,
        grid=(bn, D_GROUP_WIDTH // LANES),
        in_specs=specs,
        out_specs=pl.BlockSpec((1, s, LANES), lambda b, hp: (b, 0, hp)),
        out_shape=jax.ShapeDtypeStruct((bn, s, D_GROUP_WIDTH), BF16),
        scratch_shapes=[pltpu.VMEM((3, s, LANES), F32), pltpu.VMEM((3, s, LANES), F32)],
        compiler_params=_params(2),
        name="dilated_attention",
    )(*dqkv)


def _out_kernel(h_ref, yc_ref, yd_ref, wc_ref, wd_ref, o_ref):
    o_ref[0] = h_ref[0] + _mm(yc_ref[0], wc_ref[...]) + _mm(yd_ref[0], wd_ref[...])


def _out_proj(h, yc, yd, w_c, w_d):
    bn, s, _ = h.shape
    tm = TOKEN_TILE
    main = pl.BlockSpec((1, tm, D_MODEL), lambda b, i: (b, i, 0))
    return pl.pallas_call(
        _out_kernel,
        grid=(bn, s // tm),
        in_specs=[main, pl.BlockSpec((1, tm, C_WIDTH), lambda b, i: (b, i, 0)),
                  pl.BlockSpec((1, tm, D_GROUP_WIDTH), lambda b, i: (b, i, 0)),
                  _const_spec(w_c.shape), _const_spec(w_d.shape)],
        out_specs=main,
        out_shape=jax.ShapeDtypeStruct(h.shape, F32),
        compiler_params=_params(2),
        name="odd_out_proj",
    )(h, yc, yd, w_c, w_d)


def kernel(x, positions, norm_mix, norm_ffn, final_norm, even_w_in, gmlp_v_gain, gmlp_w_s, gmlp_b_s, pool_w,
           pool_scale, even_w_out, odd_w_in, lambda_q1, lambda_k1, lambda_q2, lambda_k2, subln_gain, odd_w_out,
           ffn_w_up, ffn_conv_w, ffn_conv_b, ffn_w_down):
    row = lambda a: a.reshape(1, -1)
    bf = lambda a: a.astype(BF16)

    def ffn(h, layer, final):
        return _conv_ffn(h, row(norm_ffn[layer]), bf(ffn_w_up[layer]), ffn_conv_w[layer].reshape(3, D_FF),
                         row(ffn_conv_b[layer]), bf(ffn_w_down[layer]), row(final_norm), final)

    h = _even_mixer(x, row(norm_mix[0]), bf(even_w_in[0]), row(gmlp_v_gain[0]), gmlp_w_s[0], gmlp_b_s[0].T,
                    bf(pool_w[0]), row(pool_scale[0]), bf(even_w_out[0]))
    h = ffn(h, 0, False)

    w_in = bf(odd_w_in[0])
    dq0 = 3 * C_WIDTH
    dwidth = 3 * D_GROUP_WIDTH

    def group_cols(g):
        return jnp.concatenate([w_in[:, dq0 + t * dwidth + g * D_GROUP_WIDTH:dq0 + t * dwidth + (g + 1) * D_GROUP_WIDTH]
                                for t in range(3)], axis=1)

    w_nat = jnp.concatenate([w_in[:, C_WIDTH:2 * C_WIDTH], group_cols(0)], axis=1)
    inv = ROPE_THETA ** (-jnp.arange(0, ROPE_DIM, 2, dtype=F32) / ROPE_DIM)
    qkv = _qkv_proj(h, positions, row(norm_mix[1]), inv.reshape(-1, 1), w_in[:, :C_WIDTH].T,
                    w_in[:, 2 * C_WIDTH:3 * C_WIDTH].T, w_nat, group_cols(1), group_cols(2))
    lam_rows = jnp.stack([lambda_q1[0], lambda_k1[0], lambda_q2[0], lambda_k2[0]])
    yc = _diff_attention(qkv[0], qkv[1], qkv[2], lam_rows, subln_gain[0].reshape(-1, 1))
    yd = _dilated_attention(qkv[3:])
    w_out = bf(odd_w_out[0])
    h = _out_proj(h, yc, yd, w_out[:C_WIDTH], w_out[C_WIDTH:])
    return ffn(h, 1, True)
```

```python
import functools
import math

import jax
import jax.numpy as jnp
from jax import lax
from jax.experimental import pallas as pl
from jax.experimental.pallas import tpu as pltpu

F32 = jnp.float32
BF16 = jnp.bfloat16

D_MODEL = 1024
A_WIDTH = 512
A_GROUPS = 4
CHUNK = 128
B_WIDTH = 512
POOL_WINDOWS = (2, 4, 8, 16)
HEAD_DIM = 64
C_HEADS = 6
C_WIDTH = 768
D_PATTERNS = ((128, 1), (512, 4), (2048, 16))
D_GROUP_WIDTH = 256
ROPE_THETA = 500000.0
ROPE_DIM = 16
D_FF = 2816
EPS = 1e-6
LAMBDA_INIT = 0.8 - 0.6 * math.exp(-0.3 * 1)

LANES = 128
HALO = 16
TOKEN_TILE = 512
FF_CHUNK = 256
ATT_Q = 512
ATT_K = 256
MXU_COLS = 256
V_ROWS = 144
DIL_BATCH = 8
DIL_BLOCK = 128
VMEM_LIMIT = 56 * 1024 * 1024
NEG_BIG = -1e30


def _rms(x, g):
    return x * lax.rsqrt(jnp.mean(x * x, axis=-1, keepdims=True) + EPS) * g


def _gelu(x):
    c = math.sqrt(2.0 / math.pi)
    return 0.5 * x * (1.0 + jnp.tanh(c * (x + 0.044715 * (x * x * x))))


def _mm(a, b):
    return jnp.dot(a, b, preferred_element_type=F32)


def _mm_nt(a, b):
    return lax.dot_general(a, b, (((1,), (1,)), ((), ())), preferred_element_type=F32)


def _const_spec(shape):
    nd = len(shape)
    return pl.BlockSpec(shape, lambda *_: (0,) * nd, pipeline_mode=pl.Buffered(1))


def _layer_spec(shape, layer):
    nd = len(shape)
    return pl.BlockSpec((None,) + tuple(shape[1:]), lambda *_: (layer,) + (0,) * (nd - 1),
                        pipeline_mode=pl.Buffered(1))


def _params(semantics):
    return pltpu.CompilerParams(dimension_semantics=semantics, vmem_limit_bytes=VMEM_LIMIT)


def _even_kernel(h_ref, halo_ref, g_ref, win_ref, vg_ref, ws_ref, bs_ref, pw_ref, ps_ref, y_ref, xs_ref, pb_ref):
    tm = h_ref.shape[1]
    i = pl.program_id(1)
    g = g_ref[...]
    xh = jnp.where(i == 0, 0.0, halo_ref[0])
    xs_ref[0:HALO, :] = _rms(xh, g).astype(BF16)
    xs_ref[HALO:, :] = _rms(h_ref[0], g).astype(BF16)

    z = _gelu(_mm(xs_ref[HALO:, :], win_ref[:, :2 * A_WIDTH]))
    u = z[:, :A_WIDTH]
    v = _rms(z[:, A_WIDTH:], vg_ref[...]).astype(BF16)
    n_chunks = tm // CHUNK
    row = lax.broadcasted_iota(jnp.int32, (CHUNK, CHUNK), 0)
    col = lax.broadcasted_iota(jnp.int32, (CHUNK, CHUNK), 1)
    for grp in range(A_GROUPS):
        lanes = slice(grp * LANES, (grp + 1) * LANES)
        w = jnp.where(row >= col, ws_ref[grp], 0.0).astype(BF16)
        rhs = jnp.concatenate([v[c * CHUNK:(c + 1) * CHUNK, lanes] for c in range(n_chunks)], axis=1)
        mixed = _mm(w, rhs) + bs_ref[:, grp:grp + 1]
        for c in range(n_chunks):
            rows = slice(c * CHUNK, (c + 1) * CHUNK)
            y_ref[0, rows, lanes] = (u[rows, lanes] * mixed[:, c * CHUNK:(c + 1) * CHUNK]).astype(BF16)

    pb_ref[...] = _mm(xs_ref[...], win_ref[:, 2 * A_WIDTH:])
    pos = i * tm + lax.broadcasted_iota(jnp.int32, (tm, 1), 0)
    for grp, window in enumerate(POOL_WINDOWS):
        lanes = slice(grp * LANES, (grp + 1) * LANES)
        acc = pb_ref[:, lanes]
        span = 1
        while span < window:
            acc = acc + pltpu.roll(acc, span, 0)
            span *= 2
        inv_count = 1.0 / jnp.minimum(pos + 1, window).astype(F32)
        pooled = (acc[HALO:] * inv_count - pb_ref[HALO:, lanes]).astype(BF16)
        yb = _mm(pooled, pw_ref[grp]) * ps_ref[:, lanes]
        y_ref[0, :, A_WIDTH + grp * LANES:A_WIDTH + (grp + 1) * LANES] = yb.astype(BF16)


def _even_mixer(h, gain, w_in, v_gain, w_s, b_s, pool_w, pool_scale):
    bn, s, _ = h.shape
    tm = TOKEN_TILE
    main = pl.BlockSpec((1, tm, D_MODEL), lambda b, i: (b, i, 0))
    halo = pl.BlockSpec((1, HALO, D_MODEL), lambda b, i: (b, jnp.maximum(i * (tm // HALO) - 1, 0), 0))
    return pl.pallas_call(
        _even_kernel,
        grid=(bn, s // tm),
        in_specs=[main, halo, _const_spec((1, D_MODEL)), _const_spec(w_in.shape), _const_spec((1, A_WIDTH)),
                  _const_spec(w_s.shape), _const_spec(b_s.shape), _const_spec(pool_w.shape),
                  _const_spec((1, B_WIDTH))],
        out_specs=main,
        out_shape=jax.ShapeDtypeStruct(h.shape, BF16),
        scratch_shapes=[pltpu.VMEM((tm + HALO, D_MODEL), BF16), pltpu.VMEM((tm + HALO, B_WIDTH), F32)],
        compiler_params=_params(("parallel", "parallel")),
        name="even_mixer",
    )(h, h, gain, w_in, v_gain, w_s, b_s, pool_w, pool_scale)


def _ffn_kernel(h_ref, ya_ref, yb_ref, wo_ref, g_ref, wup_ref, cw_ref, cb_ref, wdown_ref, fg_ref, o_ref,
                xs_ref, a_ref, hg_ref, *, final_norm):
    tm = h_ref.shape[1]
    wa = ya_ref.shape[2]
    i = pl.program_id(1)

    @pl.when(i == 0)
    def _():
        xs_ref[0:HALO, :] = jnp.zeros((HALO, D_MODEL), BF16)

    @pl.when(i > 0)
    def _():
        xs_ref[0:HALO, :] = xs_ref[tm:tm + HALO, :]

    hm = h_ref[0] + _mm(ya_ref[0], wo_ref[0:wa, :]) + _mm(yb_ref[0], wo_ref[wa:, :])
    xs_ref[HALO:, :] = _rms(hm, g_ref[...]).astype(BF16)
    for c in range(D_FF // FF_CHUNK):
        cols = slice(c * FF_CHUNK, (c + 1) * FF_CHUNK)
        a_ref[...] = _mm(xs_ref[...], wup_ref[:, cols])
        gate = _mm(xs_ref[HALO:, :], wup_ref[:, D_FF + c * FF_CHUNK:D_FF + (c + 1) * FF_CHUNK])
        conv = (a_ref[pl.ds(HALO - 2, tm), :] * cw_ref[0:1, cols]
                + a_ref[pl.ds(HALO - 1, tm), :] * cw_ref[1:2, cols]
                + a_ref[pl.ds(HALO, tm), :] * cw_ref[2:3, cols]
                + cb_ref[:, cols])
        hg_ref[:, cols] = (_gelu(conv) * gate).astype(BF16)
    y = hm + _mm(hg_ref[...], wdown_ref[...])
    if final_norm:
        y = _rms(y, fg_ref[...])
    o_ref[0] = y


def _mix_ffn(h, ya, yb, col_a, col_b, w_out, gains, w_up, conv_w, conv_b, w_down, final_gain, layer, final_norm):
    bn, s, _ = h.shape
    tm = TOKEN_TILE
    wa, wb = ya.shape[2] // (col_a[1]), yb.shape[2] // (col_b[1])
    main = pl.BlockSpec((1, tm, D_MODEL), lambda b, i: (b, i, 0))
    return pl.pallas_call(
        functools.partial(_ffn_kernel, final_norm=final_norm),
        grid=(bn, s // tm),
        in_specs=[main,
                  pl.BlockSpec((1, tm, wa), lambda b, i: (b, i, col_a[0])),
                  pl.BlockSpec((1, tm, wb), lambda b, i: (b, i, col_b[0])),
                  _const_spec(w_out.shape), _layer_spec(gains.shape, layer), _layer_spec(w_up.shape, layer),
                  _layer_spec(conv_w.shape, layer), _layer_spec(conv_b.shape, layer),
                  _layer_spec(w_down.shape, layer), _const_spec((1, D_MODEL))],
        out_specs=main,
        out_shape=jax.ShapeDtypeStruct(h.shape, F32),
        scratch_shapes=[pltpu.VMEM((tm + HALO, D_MODEL), BF16), pltpu.VMEM((tm + HALO, FF_CHUNK), F32),
                        pltpu.VMEM((tm, D_FF), BF16)],
        compiler_params=_params(("parallel", "arbitrary")),
        name="mix_ffn",
    )(h, ya, yb, w_out, gains, w_up, conv_w, conv_b, w_down, final_gain)


def _rope(x, tab):
    c, s_up, s_lo = tab
    return x * c + pltpu.roll(x, ROPE_DIM // 2, 1) * s_up + pltpu.roll(x, LANES - ROPE_DIM // 2, 1) * s_lo


def _qkv_kernel(h_ref, pos_ref, g_ref, inv_ref, wqt_ref, wvt_ref, wn_ref,
                cqt_ref, ck_ref, cvt_ref, dq0_ref, dk0_ref, dv0_ref, dq1_ref, dk1_ref, dv1_ref,
                dq2_ref, dk2_ref, dv2_ref, xs_ref, tab_ref, perm_ref):
    tm = h_ref.shape[1]
    half = ROPE_DIM // 2
    xs_ref[...] = _rms(h_ref[0], g_ref[...]).astype(BF16)

    ang = inv_ref[...] * pos_ref[0].astype(F32)
    cos_t, sin_t = jnp.cos(ang), jnp.sin(ang)
    rest = HEAD_DIM - ROPE_DIM
    ones, z_half, z_rest = jnp.ones((rest, tm), F32), jnp.zeros((half, tm), F32), jnp.zeros((rest, tm), F32)
    tab_ref[0] = jnp.concatenate([cos_t, cos_t, ones] * 2, axis=0).T
    tab_ref[1] = jnp.concatenate([z_half, sin_t, z_rest] * 2, axis=0).T
    tab_ref[2] = jnp.concatenate([-sin_t, z_half, z_rest] * 2, axis=0).T

    qt = _mm_nt(wqt_ref[...], xs_ref[...])
    pieces = []
    for hd in range(C_WIDTH // HEAD_DIM):
        x1 = qt[hd * HEAD_DIM:hd * HEAD_DIM + half]
        x2 = qt[hd * HEAD_DIM + half:hd * HEAD_DIM + ROPE_DIM]
        pieces += [x1 * cos_t - x2 * sin_t, x2 * cos_t + x1 * sin_t, qt[hd * HEAD_DIM + ROPE_DIM:(hd + 1) * HEAD_DIM]]
    qt = jnp.concatenate(pieces, axis=0) * (HEAD_DIM ** -0.5 * math.log2(math.e))
    for hp in range(C_HEADS):
        cqt_ref[0, hp, 0] = qt[hp * LANES:(hp + 1) * LANES].astype(BF16)

    vt = _mm_nt(wvt_ref[...], xs_ref[...])
    for hp in range(C_HEADS):
        for j in range(tm // ATT_K):
            cvt_ref[0, hp, j] = vt[hp * LANES:(hp + 1) * LANES, j * ATT_K:(j + 1) * ATT_K].astype(BF16)

    tabs = (tab_ref[0], tab_ref[1], tab_ref[2])

    def project(col0, width, rope, scale):
        outs = []
        for c in range(width // MXU_COLS):
            r = _mm(xs_ref[...], wn_ref[:, col0 + c * MXU_COLS:col0 + (c + 1) * MXU_COLS])
            for x in (r[:, :LANES], r[:, LANES:]):
                if rope:
                    x = _rope(x, tabs)
                if scale is not None:
                    x = x * scale
                outs.append(x)
        return outs

    for c, r in enumerate(project(0, C_WIDTH, True, None)):
        ck_ref[0, :, c * LANES:(c + 1) * LANES] = r.astype(BF16)

    dscale = HEAD_DIM ** -0.5 * math.log2(math.e)
    gw = D_GROUP_WIDTH
    col0 = C_WIDTH
    slot = 0
    for (_, dil), refs in zip(D_PATTERNS, ((dq0_ref, dk0_ref, dv0_ref), (dq1_ref, dk1_ref, dv1_ref),
                                            (dq2_ref, dk2_ref, dv2_ref))):
        n = tm // dil
        for ref, rope, scale in zip(refs, (True, True, False), (dscale, None, None)):
            for c, res in enumerate(project(col0, gw, rope, scale)):
                lanes = slice(c * LANES, (c + 1) * LANES)
                if dil == 1:
                    ref[0, 0, :, lanes] = res.astype(BF16)
                else:
                    perm_ref[slot] = res
                    for r in range(dil):
                        ref[0, r, :, lanes] = perm_ref[slot, pl.ds(r, n, stride=dil), :].astype(BF16)
                    slot += 1
            col0 += gw


def _qkv_proj(h, positions, gain, inv_col, wq_t, wv_t, w_nat):
    bn, s, _ = h.shape
    tm = TOKEN_TILE
    main = pl.BlockSpec((1, tm, D_MODEL), lambda b, i: (b, i, 0))
    pos_spec = pl.BlockSpec((1, 1, tm), lambda b, i: (b, 0, i))
    out_shapes = [jax.ShapeDtypeStruct((bn, C_HEADS, s // ATT_Q, LANES, ATT_Q), BF16),
                  jax.ShapeDtypeStruct((bn, s, C_WIDTH), BF16),
                  jax.ShapeDtypeStruct((bn, C_HEADS, s // ATT_K, LANES, ATT_K), BF16)]
    out_specs = [pl.BlockSpec((1, C_HEADS, tm // ATT_Q, LANES, ATT_Q), lambda b, i: (b, 0, i, 0, 0)),
                 pl.BlockSpec((1, tm, C_WIDTH), lambda b, i: (b, i, 0)),
                 pl.BlockSpec((1, C_HEADS, tm // ATT_K, LANES, ATT_K), lambda b, i: (b, 0, i, 0, 0))]
    n_perm = 0
    for _, dil in D_PATTERNS:
        out_shapes += [jax.ShapeDtypeStruct((bn, dil, s // dil, D_GROUP_WIDTH), BF16)] * 3
        out_specs += [pl.BlockSpec((1, dil, tm // dil, D_GROUP_WIDTH), lambda b, i: (b, 0, i, 0))] * 3
        n_perm += 0 if dil == 1 else 3 * D_GROUP_WIDTH // LANES
    return pl.pallas_call(
        _qkv_kernel,
        grid=(bn, s // tm),
        in_specs=[main, pos_spec, _const_spec((1, D_MODEL)), _const_spec(inv_col.shape), _const_spec(wq_t.shape),
                  _const_spec(wv_t.shape), _const_spec(w_nat.shape)],
        out_specs=out_specs,
        out_shape=out_shapes,
        scratch_shapes=[pltpu.VMEM((tm, D_MODEL), BF16), pltpu.VMEM((3, tm, LANES), F32),
                        pltpu.VMEM((n_perm, tm, LANES), F32)],
        compiler_params=_params(("parallel", "parallel")),
        name="qkv_proj",
    )(h, positions.reshape(bn, 1, s), gain, inv_col, wq_t, wv_t, w_nat)


def _diff_kernel(qt_ref, k_ref, vt_ref, lam_ref, sg_ref, o_ref, qc_ref, s_ref, e_ref, m_ref, acc_ref):
    tq, tk = ATT_Q, ATT_K
    n_strips = 2 * tq // LANES
    all_strips = tuple(range(n_strips))
    late = tuple(j for j in all_strips if (j * LANES) % tq >= tk)
    ones = jnp.ones((V_ROWS - LANES, tk), BF16)
    row = lax.broadcasted_iota(jnp.int32, (LANES, tq), 0)
    lam_rows = lam_ref[...]
    lam = (jnp.exp(jnp.sum(lam_rows[0:1] * lam_rows[1:2], axis=-1, keepdims=True))
           - jnp.exp(jnp.sum(lam_rows[2:3] * lam_rows[3:4], axis=-1, keepdims=True)) + LAMBDA_INIT)
    out_gain = sg_ref[...] * (1.0 - LAMBDA_INIT)

    def qk(kb, buf, strips=all_strips):
        start = pl.multiple_of(kb * tk, tk)
        rhs = qc_ref[...] if strips == all_strips else jnp.concatenate(
            [qc_ref[:, j * LANES:(j + 1) * LANES] for j in strips], axis=1)
        s = _mm(k_ref[0, pl.ds(start, tk), :], rhs)
        for idx, j in enumerate(strips):
            s_ref[buf, j] = s[:, idx * LANES:(idx + 1) * LANES]

    def soft(buf, kb, strips=all_strips, key_off=None):
        alphas = []
        for j in strips:
            cols = slice(j * LANES, (j + 1) * LANES)
            s = s_ref[buf, j]
            q_rel = (j * LANES) % tq
            if key_off is not None and q_rel < key_off + tk:
                key = key_off + lax.broadcasted_iota(jnp.int32, (tk, LANES), 0)
                qry = q_rel + lax.broadcasted_iota(jnp.int32, (tk, LANES), 1)
                s = jnp.where(key <= qry, s, NEG_BIG)
            m_old = m_ref[:, cols]
            m_new = jnp.maximum(m_old, jnp.max(s, axis=0, keepdims=True))
            alphas.append(jnp.exp2(m_old - m_new))
            m_ref[:, cols] = m_new
            e_ref[j] = jnp.exp2(s - m_new).astype(BF16)
        lhs = jnp.concatenate([vt_ref[0, 0, kb], ones], axis=0)
        pv = _mm(lhs, jnp.concatenate([e_ref[j] for j in strips], axis=1))
        if strips == all_strips:
            acc_ref[...] = acc_ref[...] * jnp.concatenate(alphas, axis=1) + pv
        else:
            for idx, j in enumerate(strips):
                cols = slice(j * LANES, (j + 1) * LANES)
                acc_ref[:, cols] = acc_ref[:, cols] * alphas[idx] + pv[:, idx * LANES:(idx + 1) * LANES]

    def query_block(qj, _):
        qt = qt_ref[0, 0, qj]
        zero = jnp.zeros_like(qt)
        qc_ref[:, :tq] = jnp.where(row < HEAD_DIM, qt, zero)
        qc_ref[:, tq:] = jnp.where(row >= HEAD_DIM, qt, zero)
        m_ref[...] = jnp.full(m_ref.shape, NEG_BIG, F32)
        acc_ref[...] = jnp.zeros(acc_ref.shape, F32)
        qk(0, 0)

        def pair(it, _):
            qk(2 * it + 1, 1)
            soft(0, 2 * it)
            qk(2 * it + 2, 0)
            soft(1, 2 * it + 1)
            return 0

        lax.fori_loop(0, qj, pair, 0)
        qk(2 * qj + 1, 1, late)
        soft(0, 2 * qj, key_off=0)
        soft(1, 2 * qj + 1, late, key_off=tk)

        inv = 1.0 / acc_ref[LANES:LANES + 1, :]
        ot = acc_ref[0:LANES, :tq] * inv[:, :tq] - lam * (acc_ref[0:LANES, tq:] * inv[:, tq:])
        ot = ot * lax.rsqrt(jnp.mean(ot * ot, axis=0, keepdims=True) + EPS) * out_gain
        o_ref[0, pl.ds(pl.multiple_of(qj * tq, tq), tq), :] = ot.T.astype(BF16)
        return 0

    lax.fori_loop(0, qt_ref.shape[2], query_block, 0)


def _diff_attention(cqt, ck, cvt, lam_rows, subln_gain_col):
    bn, s, _ = ck.shape
    tq, tk = ATT_Q, ATT_K
    return pl.pallas_call(
        _diff_kernel,
        grid=(bn, C_HEADS),
        in_specs=[pl.BlockSpec((1, 1, s // tq, LANES, tq), lambda b, h: (b, h, 0, 0, 0)),
                  pl.BlockSpec((1, s, LANES), lambda b, h: (b, 0, h)),
                  pl.BlockSpec((1, 1, s // tk, LANES, tk), lambda b, h: (b, h, 0, 0, 0)),
                  pl.BlockSpec(lam_rows.shape, lambda b, h: (0, 0)),
                  pl.BlockSpec((LANES, 1), lambda b, h: (0, 0))],
        out_specs=pl.BlockSpec((1, s, LANES), lambda b, h: (b, 0, h)),
        out_shape=jax.ShapeDtypeStruct((bn, s, C_WIDTH), BF16),
        scratch_shapes=[pltpu.VMEM((LANES, 2 * tq), BF16), pltpu.VMEM((2, 2 * tq // LANES, tk, LANES), F32),
                        pltpu.VMEM((2 * tq // LANES, tk, LANES), BF16), pltpu.VMEM((1, 2 * tq), F32),
                        pltpu.VMEM((V_ROWS, 2 * tq), F32)],
        compiler_params=_params(("parallel", "parallel")),
        name="diff_attention",
    )(cqt, ck, cvt, lam_rows, subln_gain_col)


def _dil_kernel(q0, k0, v0, q1, k1, v1, q2, k2, v2, o_ref, os_ref, ls_ref):
    nk = DIL_BLOCK
    s_len = o_ref.shape[1]
    lane = lax.broadcasted_iota(jnp.int32, (nk, LANES), 1)
    first_head = lane < HEAD_DIM
    row = lax.broadcasted_iota(jnp.int32, (2 * nk, 2 * nk), 0) % nk
    col = lax.broadcasted_iota(jnp.int32, (2 * nk, 2 * nk), 1)
    band = ((col < nk) & (col >= row)) | ((col >= nk) & (col - nk <= row))
    ones = jnp.ones((2 * nk, LANES), BF16)

    for grp, ((_, dil), q_ref, k_ref, v_ref) in enumerate(zip(D_PATTERNS, (q0, q1, q2), (k0, k1, k2), (v0, v1, v2))):
        nb = s_len // dil // nk
        per_res = min(nb, DIL_BATCH)
        n_res = DIL_BATCH // per_res
        chunks = nb // per_res

        def batch(t, _, dil=dil, per_res=per_res, n_res=n_res, chunks=chunks, grp=grp,
                  q_ref=q_ref, k_ref=k_ref, v_ref=v_ref):
            n0 = (t % chunks) * per_res
            base = pl.multiple_of(n0 * nk, nk)
            prev = pl.multiple_of(jnp.maximum(n0 - 1, 0) * nk, nk)
            for rr in range(n_res):
                r = (t // chunks) * n_res + rr
                q_all = q_ref[0, r, pl.ds(base, per_res * nk), :]
                k_all = jnp.concatenate([k_ref[0, r, pl.ds(prev, nk), :], k_ref[0, r, pl.ds(base, per_res * nk), :]], axis=0)
                v_all = jnp.concatenate([v_ref[0, r, pl.ds(prev, nk), :], v_ref[0, r, pl.ds(base, per_res * nk), :]], axis=0)
                for i in range(per_res):
                    q = q_all[i * nk:(i + 1) * nk]
                    k = k_all[i * nk:(i + 2) * nk]
                    v = jnp.concatenate([v_all[i * nk:(i + 2) * nk], ones], axis=1)
                    mask = band
                    if i == 0:
                        mask = band & (col >= jnp.where(n0 == 0, nk, 0))
                    zero = jnp.zeros_like(q)
                    q2 = jnp.concatenate([jnp.where(first_head, q, zero), jnp.where(first_head, zero, q)], axis=0)
                    s = jnp.where(mask, _mm_nt(q2, k), NEG_BIG)
                    m = jnp.max(s, axis=-1, keepdims=True)
                    pv = _mm(jnp.exp2(s - m).astype(BF16), v)
                    num = jnp.where(first_head, pv[:nk, :LANES], pv[nk:, :LANES])
                    den = jnp.where(first_head, pv[:nk, LANES:], pv[nk:, LANES:])
                    top = jnp.where(first_head, jnp.broadcast_to(m[:nk], (nk, LANES)), jnp.broadcast_to(m[nk:], (nk, LANES)))
                    rows = pl.ds((n0 + i) * (nk * dil) + r, nk, stride=dil)
                    os_ref[grp, rows, :] = num / den
                    ls_ref[grp, rows, :] = top + jnp.log2(den)
            return 0

        lax.fori_loop(0, (dil // n_res) * chunks, batch, 0)

    tile = 512
    for t in range(s_len // tile):
        rows = slice(t * tile, (t + 1) * tile)
        lse = [ls_ref[grp, rows, :] for grp in range(3)]
        top = jnp.maximum(jnp.maximum(lse[0], lse[1]), lse[2])
        wts = [jnp.exp2(x - top) for x in lse]
        num = wts[0] * os_ref[0, rows, :] + wts[1] * os_ref[1, rows, :] + wts[2] * os_ref[2, rows, :]
        o_ref[0, rows, :] = (num / (wts[0] + wts[1] + wts[2])).astype(BF16)


def _dilated_attention(dqkv):
    bn = dqkv[0].shape[0]
    s = dqkv[0].shape[2]
    specs = []
    for (_, dil) in D_PATTERNS:
        specs += [pl.BlockSpec((1, dil, s // dil, LANES), lambda b, hp: (b, 0, 0, hp))] * 3
    return pl.pallas_call(
        _dil_kernel,
        grid=(bn, D_GROUP_WIDTH // LANES),
        in_specs=specs,
        out_specs=pl.BlockSpec((1, s, LANES), lambda b, hp: (b, 0, hp)),
        out_shape=jax.ShapeDtypeStruct((bn, s, D_GROUP_WIDTH), BF16),
        scratch_shapes=[pltpu.VMEM((3, s, LANES), F32), pltpu.VMEM((3, s, LANES), F32)],
        compiler_params=_params(("parallel", "parallel")),
        name="dilated_attention",
    )(*dqkv)


def kernel(x, positions, norm_mix, norm_ffn, final_norm, even_w_in, gmlp_v_gain, gmlp_w_s, gmlp_b_s, pool_w,
           pool_scale, even_w_out, odd_w_in, lambda_q1, lambda_k1, lambda_q2, lambda_k2, subln_gain, odd_w_out,
           ffn_w_up, ffn_conv_w, ffn_conv_b, ffn_w_down):
    row = lambda a: a.reshape(1, -1)
    bf = lambda a: a.astype(BF16)
    depth = norm_ffn.shape[0]
    ffn_params = (norm_ffn.reshape(depth, 1, D_MODEL), bf(ffn_w_up), ffn_conv_w.reshape(depth, 3, D_FF),
                  ffn_conv_b.reshape(depth, 1, D_FF), bf(ffn_w_down), row(final_norm))

    y = _even_mixer(x, row(norm_mix[0]), bf(even_w_in[0]), row(gmlp_v_gain[0]), gmlp_w_s[0], gmlp_b_s[0].T,
                    bf(pool_w[0]), row(pool_scale[0]))
    h = _mix_ffn(x, y, y, (0, 2), (1, 2), bf(even_w_out[0]), *ffn_params, layer=0, final_norm=False)

    w_in = bf(odd_w_in[0])
    dq0 = 3 * C_WIDTH
    dwidth = 3 * D_GROUP_WIDTH

    def group_cols(g):
        return [w_in[:, dq0 + t * dwidth + g * D_GROUP_WIDTH:dq0 + t * dwidth + (g + 1) * D_GROUP_WIDTH]
                for t in range(3)]

    w_nat = jnp.concatenate([w_in[:, C_WIDTH:2 * C_WIDTH]] + group_cols(0) + group_cols(1) + group_cols(2), axis=1)
    inv = ROPE_THETA ** (-jnp.arange(0, ROPE_DIM, 2, dtype=F32) / ROPE_DIM)
    qkv = _qkv_proj(h, positions, row(norm_mix[1]), inv.reshape(-1, 1), w_in[:, :C_WIDTH].T,
                    w_in[:, 2 * C_WIDTH:3 * C_WIDTH].T, w_nat)
    lam_rows = jnp.stack([lambda_q1[0], lambda_k1[0], lambda_q2[0], lambda_k2[0]])
    yc = _diff_attention(qkv[0], qkv[1], qkv[2], lam_rows, subln_gain[0].reshape(-1, 1))
    yd = _dilated_attention(qkv[3:])
    return _mix_ffn(h, yc, yd, (0, 1), (0, 1), bf(odd_w_out[0]), *ffn_params, layer=1, final_norm=True)
```

```python
import functools
import math

import jax
import jax.numpy as jnp
from jax import lax
from jax.experimental import pallas as pl
from jax.experimental.pallas import tpu as pltpu

F32 = jnp.float32
BF16 = jnp.bfloat16

D_MODEL = 1024
A_WIDTH = 512
A_GROUPS = 4
CHUNK = 128
B_WIDTH = 512
POOL_WINDOWS = (2, 4, 8, 16)
HEAD_DIM = 64
C_HEADS = 6
C_WIDTH = 768
D_PATTERNS = ((128, 1), (512, 4), (2048, 16))
D_GROUP_WIDTH = 256
ROPE_THETA = 500000.0
ROPE_DIM = 16
D_FF = 2816
EPS = 1e-6
LAMBDA_INIT = 0.8 - 0.6 * math.exp(-0.3 * 1)

LANES = 128
HALO = 16
TOKEN_TILE = 512
FF_CHUNK = 256
ATT_Q = 512
ATT_K = 256
MXU_COLS = 256
V_ROWS = 144
DIL_BATCH = 16
DIL_BLOCK = 128
VMEM_LIMIT = 56 * 1024 * 1024
NEG_BIG = -1e30


def _rms(x, g):
    return x * lax.rsqrt(jnp.mean(x * x, axis=-1, keepdims=True) + EPS) * g


def _gelu(x):
    c = math.sqrt(2.0 / math.pi)
    return 0.5 * x * (1.0 + jnp.tanh(c * (x + 0.044715 * (x * x * x))))


def _mm(a, b):
    return jnp.dot(a, b, preferred_element_type=F32)


def _mm_nt(a, b):
    return lax.dot_general(a, b, (((1,), (1,)), ((), ())), preferred_element_type=F32)


def _const_spec(shape):
    nd = len(shape)
    return pl.BlockSpec(shape, lambda *_: (0,) * nd, pipeline_mode=pl.Buffered(1))


def _layer_spec(shape, layer):
    nd = len(shape)
    return pl.BlockSpec((None,) + tuple(shape[1:]), lambda *_: (layer,) + (0,) * (nd - 1),
                        pipeline_mode=pl.Buffered(1))


def _params(semantics):
    return pltpu.CompilerParams(dimension_semantics=semantics, vmem_limit_bytes=VMEM_LIMIT)


def _even_kernel(h_ref, halo_ref, g_ref, win_ref, vg_ref, ws_ref, bs_ref, pw_ref, ps_ref, *refs):
    n_cast = (len(refs) - 3) // 2
    cast_in, y_ref, cast_out = refs[:n_cast], refs[n_cast], refs[n_cast + 1:2 * n_cast + 1]
    xs_ref, pb_ref = refs[2 * n_cast + 1:]
    for src, dst in zip(cast_in, cast_out):
        dst[...] = src[...].astype(BF16)
    tm = h_ref.shape[1]
    i = pl.program_id(1)
    g = g_ref[...]
    xh = jnp.where(i == 0, 0.0, halo_ref[0])
    xs_ref[0:HALO, :] = _rms(xh, g).astype(BF16)
    xs_ref[HALO:, :] = _rms(h_ref[0], g).astype(BF16)

    z = _gelu(_mm(xs_ref[HALO:, :], win_ref[:, :2 * A_WIDTH]))
    u = z[:, :A_WIDTH]
    v = _rms(z[:, A_WIDTH:], vg_ref[...]).astype(BF16)
    n_chunks = tm // CHUNK
    row = lax.broadcasted_iota(jnp.int32, (CHUNK, CHUNK), 0)
    col = lax.broadcasted_iota(jnp.int32, (CHUNK, CHUNK), 1)
    for grp in range(A_GROUPS):
        lanes = slice(grp * LANES, (grp + 1) * LANES)
        w = jnp.where(row >= col, ws_ref[grp], 0.0).astype(BF16)
        rhs = jnp.concatenate([v[c * CHUNK:(c + 1) * CHUNK, lanes] for c in range(n_chunks)], axis=1)
        mixed = _mm(w, rhs) + bs_ref[:, grp:grp + 1]
        for c in range(n_chunks):
            rows = slice(c * CHUNK, (c + 1) * CHUNK)
            y_ref[0, rows, lanes] = (u[rows, lanes] * mixed[:, c * CHUNK:(c + 1) * CHUNK]).astype(BF16)

    pb_ref[...] = _mm(xs_ref[...], win_ref[:, 2 * A_WIDTH:])
    pos = i * tm + lax.broadcasted_iota(jnp.int32, (tm, 1), 0)
    for grp, window in enumerate(POOL_WINDOWS):
        lanes = slice(grp * LANES, (grp + 1) * LANES)
        acc = pb_ref[:, lanes]
        span = 1
        while span < window:
            acc = acc + pltpu.roll(acc, span, 0)
            span *= 2
        inv_count = 1.0 / jnp.minimum(pos + 1, window).astype(F32)
        pooled = (acc[HALO:] * inv_count - pb_ref[HALO:, lanes]).astype(BF16)
        yb = _mm(pooled, pw_ref[grp]) * ps_ref[:, lanes]
        y_ref[0, :, A_WIDTH + grp * LANES:A_WIDTH + (grp + 1) * LANES] = yb.astype(BF16)


def _even_mixer(h, gain, w_in, v_gain, w_s, b_s, pool_w, pool_scale, later_weights):
    bn, s, _ = h.shape
    tm = TOKEN_TILE
    n_steps = bn * (s // tm)
    main = pl.BlockSpec((1, tm, D_MODEL), lambda b, i: (b, i, 0))
    halo = pl.BlockSpec((1, HALO, D_MODEL), lambda b, i: (b, jnp.maximum(i * (tm // HALO) - 1, 0), 0))
    slabs = [pl.BlockSpec((w.shape[0] // n_steps, w.shape[1]), lambda b, i: (b * (s // tm) + i, 0))
             for w in later_weights]
    outs = pl.pallas_call(
        _even_kernel,
        grid=(bn, s // tm),
        in_specs=[main, halo, _const_spec((1, D_MODEL)), _const_spec(w_in.shape), _const_spec((1, A_WIDTH)),
                  _const_spec(w_s.shape), _const_spec(b_s.shape), _const_spec(pool_w.shape),
                  _const_spec((1, B_WIDTH))] + slabs,
        out_specs=[main] + slabs,
        out_shape=[jax.ShapeDtypeStruct(h.shape, BF16)] + [jax.ShapeDtypeStruct(w.shape, BF16) for w in later_weights],
        scratch_shapes=[pltpu.VMEM((tm + HALO, D_MODEL), BF16), pltpu.VMEM((tm + HALO, B_WIDTH), F32)],
        compiler_params=_params(("parallel", "parallel")),
        name="even_mixer",
    )(h, h, gain, w_in, v_gain, w_s, b_s, pool_w, pool_scale, *later_weights)
    return outs[0], outs[1:]


def _ffn_kernel(h_ref, ya_ref, yb_ref, wo_ref, g_ref, wup_ref, cw_ref, cb_ref, wdown_ref, fg_ref, o_ref,
                xs_ref, a_ref, hg_ref, *, final_norm):
    tm = h_ref.shape[1]
    wa = ya_ref.shape[2]
    i = pl.program_id(1)

    @pl.when(i == 0)
    def _():
        xs_ref[0:HALO, :] = jnp.zeros((HALO, D_MODEL), BF16)

    @pl.when(i > 0)
    def _():
        xs_ref[0:HALO, :] = xs_ref[tm:tm + HALO, :]

    hm = h_ref[0] + _mm(ya_ref[0], wo_ref[0:wa, :]) + _mm(yb_ref[0], wo_ref[wa:, :])
    xs_ref[HALO:, :] = _rms(hm, g_ref[...]).astype(BF16)
    for c in range(D_FF // FF_CHUNK):
        cols = slice(c * FF_CHUNK, (c + 1) * FF_CHUNK)
        a_ref[...] = _mm(xs_ref[...], wup_ref[:, cols])
        gate = _mm(xs_ref[HALO:, :], wup_ref[:, D_FF + c * FF_CHUNK:D_FF + (c + 1) * FF_CHUNK])
        conv = (a_ref[pl.ds(HALO - 2, tm), :] * cw_ref[0:1, cols]
                + a_ref[pl.ds(HALO - 1, tm), :] * cw_ref[1:2, cols]
                + a_ref[pl.ds(HALO, tm), :] * cw_ref[2:3, cols]
                + cb_ref[:, cols])
        hg_ref[:, cols] = (_gelu(conv) * gate).astype(BF16)
    y = hm + _mm(hg_ref[...], wdown_ref[...])
    if final_norm:
        y = _rms(y, fg_ref[...])
    o_ref[0] = y


def _mix_ffn(h, ya, yb, col_a, col_b, w_out, gains, w_up, conv_w, conv_b, w_down, final_gain, layer, final_norm):
    bn, s, _ = h.shape
    tm = TOKEN_TILE
    wa, wb = ya.shape[2] // (col_a[1]), yb.shape[2] // (col_b[1])
    main = pl.BlockSpec((1, tm, D_MODEL), lambda b, i: (b, i, 0))
    return pl.pallas_call(
        functools.partial(_ffn_kernel, final_norm=final_norm),
        grid=(bn, s // tm),
        in_specs=[main,
                  pl.BlockSpec((1, tm, wa), lambda b, i: (b, i, col_a[0])),
                  pl.BlockSpec((1, tm, wb), lambda b, i: (b, i, col_b[0])),
                  _const_spec(w_out.shape), _layer_spec(gains.shape, layer), _layer_spec(w_up.shape, layer),
                  _layer_spec(conv_w.shape, layer), _layer_spec(conv_b.shape, layer),
                  _layer_spec(w_down.shape, layer), _const_spec((1, D_MODEL))],
        out_specs=main,
        out_shape=jax.ShapeDtypeStruct(h.shape, F32),
        scratch_shapes=[pltpu.VMEM((tm + HALO, D_MODEL), BF16), pltpu.VMEM((tm + HALO, FF_CHUNK), F32),
                        pltpu.VMEM((tm, D_FF), BF16)],
        compiler_params=_params(("parallel", "arbitrary")),
        name="mix_ffn",
    )(h, ya, yb, w_out, gains, w_up, conv_w, conv_b, w_down, final_gain)


def _rope(x, tab):
    c, s_up, s_lo = tab
    return x * c + pltpu.roll(x, ROPE_DIM // 2, 1) * s_up + pltpu.roll(x, LANES - ROPE_DIM // 2, 1) * s_lo


def _qkv_kernel(h_ref, pos_ref, g_ref, inv_ref, wqt_ref, wvt_ref, wn_ref,
                cqt_ref, ck_ref, cvt_ref, dq0_ref, dk0_ref, dv0_ref, dq1_ref, dk1_ref, dv1_ref,
                dq2_ref, dk2_ref, dv2_ref, xs_ref, tab_ref, perm_ref):
    tm = h_ref.shape[1]
    half = ROPE_DIM // 2
    xs_ref[...] = _rms(h_ref[0], g_ref[...]).astype(BF16)

    ang = inv_ref[...] * pos_ref[0].astype(F32)
    cos_t, sin_t = jnp.cos(ang), jnp.sin(ang)
    rest = HEAD_DIM - ROPE_DIM
    ones, z_half, z_rest = jnp.ones((rest, tm), F32), jnp.zeros((half, tm), F32), jnp.zeros((rest, tm), F32)
    tab_ref[0] = jnp.concatenate([cos_t, cos_t, ones] * 2, axis=0).T
    tab_ref[1] = jnp.concatenate([z_half, sin_t, z_rest] * 2, axis=0).T
    tab_ref[2] = jnp.concatenate([-sin_t, z_half, z_rest] * 2, axis=0).T

    qt = _mm_nt(wqt_ref[...], xs_ref[...])
    pieces = []
    for hd in range(C_WIDTH // HEAD_DIM):
        x1 = qt[hd * HEAD_DIM:hd * HEAD_DIM + half]
        x2 = qt[hd * HEAD_DIM + half:hd * HEAD_DIM + ROPE_DIM]
        pieces += [x1 * cos_t - x2 * sin_t, x2 * cos_t + x1 * sin_t, qt[hd * HEAD_DIM + ROPE_DIM:(hd + 1) * HEAD_DIM]]
    qt = jnp.concatenate(pieces, axis=0) * (HEAD_DIM ** -0.5 * math.log2(math.e))
    for hp in range(C_HEADS):
        cqt_ref[0, hp, 0] = qt[hp * LANES:(hp + 1) * LANES].astype(BF16)

    vt = _mm_nt(wvt_ref[...], xs_ref[...])
    for hp in range(C_HEADS):
        for j in range(tm // ATT_K):
            cvt_ref[0, hp, j] = vt[hp * LANES:(hp + 1) * LANES, j * ATT_K:(j + 1) * ATT_K].astype(BF16)

    tabs = (tab_ref[0], tab_ref[1], tab_ref[2])

    def project(col0, width, rope, scale):
        outs = []
        for c in range(width // MXU_COLS):
            r = _mm(xs_ref[...], wn_ref[:, col0 + c * MXU_COLS:col0 + (c + 1) * MXU_COLS])
            for x in (r[:, :LANES], r[:, LANES:]):
                if rope:
                    x = _rope(x, tabs)
                if scale is not None:
                    x = x * scale
                outs.append(x)
        return outs

    for c, r in enumerate(project(0, C_WIDTH, True, None)):
        ck_ref[0, :, c * LANES:(c + 1) * LANES] = r.astype(BF16)

    dscale = HEAD_DIM ** -0.5 * math.log2(math.e)
    gw = D_GROUP_WIDTH
    col0 = C_WIDTH
    slot = 0
    for (_, dil), refs in zip(D_PATTERNS, ((dq0_ref, dk0_ref, dv0_ref), (dq1_ref, dk1_ref, dv1_ref),
                                            (dq2_ref, dk2_ref, dv2_ref))):
        n = tm // dil
        for ref, rope, scale in zip(refs, (True, True, False), (dscale, None, None)):
            for c, res in enumerate(project(col0, gw, rope, scale)):
                lanes = slice(c * LANES, (c + 1) * LANES)
                if dil == 1:
                    ref[0, 0, :, lanes] = res.astype(BF16)
                else:
                    perm_ref[slot] = res
                    for r in range(dil):
                        ref[0, r, :, lanes] = perm_ref[slot, pl.ds(r, n, stride=dil), :].astype(BF16)
                    slot += 1
            col0 += gw


def _qkv_proj(h, positions, gain, inv_col, wq_t, wv_t, w_nat):
    bn, s, _ = h.shape
    tm = TOKEN_TILE
    main = pl.BlockSpec((1, tm, D_MODEL), lambda b, i: (b, i, 0))
    pos_spec = pl.BlockSpec((1, 1, tm), lambda b, i: (b, 0, i))
    out_shapes = [jax.ShapeDtypeStruct((bn, C_HEADS, s // ATT_Q, LANES, ATT_Q), BF16),
                  jax.ShapeDtypeStruct((bn, s, C_WIDTH), BF16),
                  jax.ShapeDtypeStruct((bn, C_HEADS, s // ATT_K, LANES, ATT_K), BF16)]
    out_specs = [pl.BlockSpec((1, C_HEADS, tm // ATT_Q, LANES, ATT_Q), lambda b, i: (b, 0, i, 0, 0)),
                 pl.BlockSpec((1, tm, C_WIDTH), lambda b, i: (b, i, 0)),
                 pl.BlockSpec((1, C_HEADS, tm // ATT_K, LANES, ATT_K), lambda b, i: (b, 0, i, 0, 0))]
    n_perm = 0
    for _, dil in D_PATTERNS:
        out_shapes += [jax.ShapeDtypeStruct((bn, dil, s // dil, D_GROUP_WIDTH), BF16)] * 3
        out_specs += [pl.BlockSpec((1, dil, tm // dil, D_GROUP_WIDTH), lambda b, i: (b, 0, i, 0))] * 3
        n_perm += 0 if dil == 1 else 3 * D_GROUP_WIDTH // LANES
    return pl.pallas_call(
        _qkv_kernel,
        grid=(bn, s // tm),
        in_specs=[main, pos_spec, _const_spec((1, D_MODEL)), _const_spec(inv_col.shape), _const_spec(wq_t.shape),
                  _const_spec(wv_t.shape), _const_spec(w_nat.shape)],
        out_specs=out_specs,
        out_shape=out_shapes,
        scratch_shapes=[pltpu.VMEM((tm, D_MODEL), BF16), pltpu.VMEM((3, tm, LANES), F32),
                        pltpu.VMEM((n_perm, tm, LANES), F32)],
        compiler_params=_params(("parallel", "parallel")),
        name="qkv_proj",
    )(h, positions.reshape(bn, 1, s), gain, inv_col, wq_t, wv_t, w_nat)


def _diff_kernel(qt_ref, k_ref, vt_ref, lam_ref, sg_ref, o_ref, qc_ref, s_ref, e_ref, m_ref, acc_ref):
    tq, tk = ATT_Q, ATT_K
    n_strips = 2 * tq // LANES
    all_strips = tuple(range(n_strips))
    late = tuple(j for j in all_strips if (j * LANES) % tq >= tk)
    ones = jnp.ones((V_ROWS - LANES, tk), BF16)
    row = lax.broadcasted_iota(jnp.int32, (LANES, tq), 0)
    lam_rows = lam_ref[...]
    lam = (jnp.exp(jnp.sum(lam_rows[0:1] * lam_rows[1:2], axis=-1, keepdims=True))
           - jnp.exp(jnp.sum(lam_rows[2:3] * lam_rows[3:4], axis=-1, keepdims=True)) + LAMBDA_INIT)
    out_gain = sg_ref[...] * (1.0 - LAMBDA_INIT)

    def qk(kb, buf, strips=all_strips):
        start = pl.multiple_of(kb * tk, tk)
        rhs = qc_ref[...] if strips == all_strips else jnp.concatenate(
            [qc_ref[:, j * LANES:(j + 1) * LANES] for j in strips], axis=1)
        s = _mm(k_ref[0, pl.ds(start, tk), :], rhs)
        for idx, j in enumerate(strips):
            s_ref[buf, j] = s[:, idx * LANES:(idx + 1) * LANES]

    def soft(buf, kb, strips=all_strips, key_off=None):
        alphas = []
        for j in strips:
            cols = slice(j * LANES, (j + 1) * LANES)
            s = s_ref[buf, j]
            q_rel = (j * LANES) % tq
            if key_off is not None and q_rel < key_off + tk:
                key = key_off + lax.broadcasted_iota(jnp.int32, (tk, LANES), 0)
                qry = q_rel + lax.broadcasted_iota(jnp.int32, (tk, LANES), 1)
                s = jnp.where(key <= qry, s, NEG_BIG)
            m_old = m_ref[:, cols]
            m_new = jnp.maximum(m_old, jnp.max(s, axis=0, keepdims=True))
            alphas.append(jnp.exp2(m_old - m_new))
            m_ref[:, cols] = m_new
            e_ref[j] = jnp.exp2(s - m_new).astype(BF16)
        lhs = jnp.concatenate([vt_ref[0, 0, kb], ones], axis=0)
        pv = _mm(lhs, jnp.concatenate([e_ref[j] for j in strips], axis=1))
        if strips == all_strips:
            acc_ref[...] = acc_ref[...] * jnp.concatenate(alphas, axis=1) + pv
        else:
            for idx, j in enumerate(strips):
                cols = slice(j * LANES, (j + 1) * LANES)
                acc_ref[:, cols] = acc_ref[:, cols] * alphas[idx] + pv[:, idx * LANES:(idx + 1) * LANES]

    def load_queries(qj):
        qt = qt_ref[0, 0, qj]
        zero = jnp.zeros_like(qt)
        qc_ref[:, :tq] = jnp.where(row < HEAD_DIM, qt, zero)
        qc_ref[:, tq:] = jnp.where(row >= HEAD_DIM, qt, zero)

    n_blocks = qt_ref.shape[2]
    load_queries(0)
    qk(0, 0)

    def query_block(qj, _):
        m_ref[...] = jnp.full(m_ref.shape, NEG_BIG, F32)
        acc_ref[...] = jnp.zeros(acc_ref.shape, F32)

        def pair(it, _):
            qk(2 * it + 1, 1)
            soft(0, 2 * it)
            qk(2 * it + 2, 0)
            soft(1, 2 * it + 1)
            return 0

        lax.fori_loop(0, qj, pair, 0)
        qk(2 * qj + 1, 1, late)
        soft(0, 2 * qj, key_off=0)
        load_queries(jnp.minimum(qj + 1, n_blocks - 1))
        qk(0, 0)
        soft(1, 2 * qj + 1, late, key_off=tk)

        inv = 1.0 / acc_ref[LANES:LANES + 1, :]
        ot = acc_ref[0:LANES, :tq] * inv[:, :tq] - lam * (acc_ref[0:LANES, tq:] * inv[:, tq:])
        ot = ot * lax.rsqrt(jnp.mean(ot * ot, axis=0, keepdims=True) + EPS) * out_gain
        o_ref[0, pl.ds(pl.multiple_of(qj * tq, tq), tq), :] = ot.T.astype(BF16)
        return 0

    lax.fori_loop(0, n_blocks, query_block, 0)


def _diff_attention(cqt, ck, cvt, lam_rows, subln_gain_col):
    bn, s, _ = ck.shape
    tq, tk = ATT_Q, ATT_K
    return pl.pallas_call(
        _diff_kernel,
        grid=(bn, C_HEADS),
        in_specs=[pl.BlockSpec((1, 1, s // tq, LANES, tq), lambda b, h: (b, h, 0, 0, 0)),
                  pl.BlockSpec((1, s, LANES), lambda b, h: (b, 0, h)),
                  pl.BlockSpec((1, 1, s // tk, LANES, tk), lambda b, h: (b, h, 0, 0, 0)),
                  pl.BlockSpec(lam_rows.shape, lambda b, h: (0, 0)),
                  pl.BlockSpec((LANES, 1), lambda b, h: (0, 0))],
        out_specs=pl.BlockSpec((1, s, LANES), lambda b, h: (b, 0, h)),
        out_shape=jax.ShapeDtypeStruct((bn, s, C_WIDTH), BF16),
        scratch_shapes=[pltpu.VMEM((LANES, 2 * tq), BF16), pltpu.VMEM((2, 2 * tq // LANES, tk, LANES), F32),
                        pltpu.VMEM((2 * tq // LANES, tk, LANES), BF16), pltpu.VMEM((1, 2 * tq), F32),
                        pltpu.VMEM((V_ROWS, 2 * tq), F32)],
        compiler_params=_params(("parallel", "parallel")),
        name="diff_attention",
    )(cqt, ck, cvt, lam_rows, subln_gain_col)


def _dil_kernel(q0, k0, v0, q1, k1, v1, q2, k2, v2, o_ref, os_ref, ls_ref):
    nk = DIL_BLOCK
    s_len = o_ref.shape[1]
    lane = lax.broadcasted_iota(jnp.int32, (nk, LANES), 1)
    first_head = lane < HEAD_DIM
    row = lax.broadcasted_iota(jnp.int32, (2 * nk, 2 * nk), 0) % nk
    col = lax.broadcasted_iota(jnp.int32, (2 * nk, 2 * nk), 1)
    band = ((col < nk) & (col >= row)) | ((col >= nk) & (col - nk <= row))
    ones = jnp.ones((2 * nk, LANES), BF16)

    for grp, ((_, dil), q_ref, k_ref, v_ref) in enumerate(zip(D_PATTERNS, (q0, q1, q2), (k0, k1, k2), (v0, v1, v2))):
        nb = s_len // dil // nk
        per_res = min(nb, DIL_BATCH)
        n_res = DIL_BATCH // per_res
        chunks = nb // per_res

        def batch(t, _, dil=dil, per_res=per_res, n_res=n_res, chunks=chunks, grp=grp,
                  q_ref=q_ref, k_ref=k_ref, v_ref=v_ref):
            n0 = (t % chunks) * per_res
            base = pl.multiple_of(n0 * nk, nk)
            prev = pl.multiple_of(jnp.maximum(n0 - 1, 0) * nk, nk)
            for rr in range(n_res):
                r = (t // chunks) * n_res + rr
                q_all = q_ref[0, r, pl.ds(base, per_res * nk), :]
                k_all = jnp.concatenate([k_ref[0, r, pl.ds(prev, nk), :], k_ref[0, r, pl.ds(base, per_res * nk), :]], axis=0)
                v_all = jnp.concatenate([v_ref[0, r, pl.ds(prev, nk), :], v_ref[0, r, pl.ds(base, per_res * nk), :]], axis=0)
                for i in range(per_res):
                    q = q_all[i * nk:(i + 1) * nk]
                    k = k_all[i * nk:(i + 2) * nk]
                    v = jnp.concatenate([v_all[i * nk:(i + 2) * nk], ones], axis=1)
                    mask = band
                    if i == 0:
                        mask = band & (col >= jnp.where(n0 == 0, nk, 0))
                    zero = jnp.zeros_like(q)
                    q2 = jnp.concatenate([jnp.where(first_head, q, zero), jnp.where(first_head, zero, q)], axis=0)
                    s = jnp.where(mask, _mm_nt(q2, k), NEG_BIG)
                    m = jnp.max(s, axis=-1, keepdims=True)
                    pv = _mm(jnp.exp2(s - m).astype(BF16), v)
                    num = jnp.where(first_head, pv[:nk, :LANES], pv[nk:, :LANES])
                    den = jnp.where(first_head, pv[:nk, LANES:], pv[nk:, LANES:])
                    top = jnp.where(first_head, jnp.broadcast_to(m[:nk], (nk, LANES)), jnp.broadcast_to(m[nk:], (nk, LANES)))
                    rows = pl.ds((n0 + i) * (nk * dil) + r, nk, stride=dil)
                    os_ref[grp, rows, :] = num / den
                    ls_ref[grp, rows, :] = top + jnp.log2(den)
            return 0

        lax.fori_loop(0, (dil // n_res) * chunks, batch, 0)

    tile = 512
    for t in range(s_len // tile):
        rows = slice(t * tile, (t + 1) * tile)
        lse = [ls_ref[grp, rows, :] for grp in range(3)]
        top = jnp.maximum(jnp.maximum(lse[0], lse[1]), lse[2])
        wts = [jnp.exp2(x - top) for x in lse]
        num = wts[0] * os_ref[0, rows, :] + wts[1] * os_ref[1, rows, :] + wts[2] * os_ref[2, rows, :]
        o_ref[0, rows, :] = (num / (wts[0] + wts[1] + wts[2])).astype(BF16)


def _dilated_attention(dqkv):
    bn = dqkv[0].shape[0]
    s = dqkv[0].shape[2]
    specs = []
    for (_, dil) in D_PATTERNS:
        specs += [pl.BlockSpec((1, dil, s // dil, LANES), lambda b, hp: (b, 0, 0, hp))] * 3
    return pl.pallas_call(
        _dil_kernel,
        grid=(bn, D_GROUP_WIDTH // LANES),
        in_specs=specs,
        out_specs=pl.BlockSpec((1, s, LANES), lambda b, hp: (b, 0, hp)),
        out_shape=jax.ShapeDtypeStruct((bn, s, D_GROUP_WIDTH), BF16),
        scratch_shapes=[pltpu.VMEM((3, s, LANES), F32), pltpu.VMEM((3, s, LANES), F32)],
        compiler_params=_params(("parallel", "parallel")),
        name="dilated_attention",
    )(*dqkv)


def kernel(x, positions, norm_mix, norm_ffn, final_norm, even_w_in, gmlp_v_gain, gmlp_w_s, gmlp_b_s, pool_w,
           pool_scale, even_w_out, odd_w_in, lambda_q1, lambda_k1, lambda_q2, lambda_k2, subln_gain, odd_w_out,
           ffn_w_up, ffn_conv_w, ffn_conv_b, ffn_w_down):
    row = lambda a: a.reshape(1, -1)
    bf = lambda a: a.astype(BF16)
    depth = norm_ffn.shape[0]

    later = [ffn_w_up.reshape(depth * D_MODEL, 2 * D_FF), ffn_w_down.reshape(depth * D_FF, D_MODEL),
             odd_w_in[0], odd_w_out[0], even_w_out[0]]
    y, (w_up, w_down, w_in, w_out_odd, w_out_even) = _even_mixer(
        x, row(norm_mix[0]), bf(even_w_in[0]), row(gmlp_v_gain[0]), gmlp_w_s[0], gmlp_b_s[0].T,
        bf(pool_w[0]), row(pool_scale[0]), later)
    ffn_params = (norm_ffn.reshape(depth, 1, D_MODEL), w_up.reshape(depth, D_MODEL, 2 * D_FF),
                  ffn_conv_w.reshape(depth, 3, D_FF), ffn_conv_b.reshape(depth, 1, D_FF),
                  w_down.reshape(depth, D_FF, D_MODEL), row(final_norm))
    h = _mix_ffn(x, y, y, (0, 2), (1, 2), w_out_even, *ffn_params, layer=0, final_norm=False)

    dq0 = 3 * C_WIDTH
    dwidth = 3 * D_GROUP_WIDTH

    def group_cols(g):
        return [w_in[:, dq0 + t * dwidth + g * D_GROUP_WIDTH:dq0 + t * dwidth + (g + 1) * D_GROUP_WIDTH]
                for t in range(3)]

    w_nat = jnp.concatenate([w_in[:, C_WIDTH:2 * C_WIDTH]] + group_cols(0) + group_cols(1) + group_cols(2), axis=1)
    inv = ROPE_THETA ** (-jnp.arange(0, ROPE_DIM, 2, dtype=F32) / ROPE_DIM)
    qkv = _qkv_proj(h, positions, row(norm_mix[1]), inv.reshape(-1, 1), w_in[:, :C_WIDTH].T,
                    w_in[:, 2 * C_WIDTH:3 * C_WIDTH].T, w_nat)
    lam_rows = jnp.stack([lambda_q1[0], lambda_k1[0], lambda_q2[0], lambda_k2[0]])
    yc = _diff_attention(qkv[0], qkv[1], qkv[2], lam_rows, subln_gain[0].reshape(-1, 1))
    yd = _dilated_attention(qkv[3:])
    return _mix_ffn(h, yc, yd, (0, 1), (0, 1), w_out_odd, *ffn_params, layer=1, final_norm=True)
```

```python
import functools
import math

import jax
import jax.numpy as jnp
from jax import lax
from jax.experimental import pallas as pl
from jax.experimental.pallas import tpu as pltpu

F32 = jnp.float32
BF16 = jnp.bfloat16

D_MODEL = 1024
A_WIDTH = 512
A_GROUPS = 4
CHUNK = 128
B_WIDTH = 512
POOL_WINDOWS = (2, 4, 8, 16)
HEAD_DIM = 64
C_HEADS = 6
C_WIDTH = 768
D_PATTERNS = ((128, 1), (512, 4), (2048, 16))
D_GROUP_WIDTH = 256
ROPE_THETA = 500000.0
ROPE_DIM = 16
D_FF = 2816
EPS = 1e-6
LAMBDA_INIT = 0.8 - 0.6 * math.exp(-0.3 * 1)

LANES = 128
HALO = 16
TOKEN_TILE = 512
FF_CHUNK = 256
ATT_Q = 512
ATT_K = 256
MXU_COLS = 256
V_ROWS = 144
DIL_BATCH = 16
DIL_BLOCK = 128
VMEM_LIMIT = 56 * 1024 * 1024
NEG_BIG = -1e30


def _rms(x, g):
    return x * lax.rsqrt(jnp.mean(x * x, axis=-1, keepdims=True) + EPS) * g


def _gelu(x):
    c = math.sqrt(2.0 / math.pi)
    return 0.5 * x * (1.0 + jnp.tanh(c * (x + 0.044715 * (x * x * x))))


def _mm(a, b):
    return jnp.dot(a, b, preferred_element_type=F32)


def _mm_nt(a, b):
    return lax.dot_general(a, b, (((1,), (1,)), ((), ())), preferred_element_type=F32)


def _const_spec(shape):
    nd = len(shape)
    return pl.BlockSpec(shape, lambda *_: (0,) * nd, pipeline_mode=pl.Buffered(1))


def _layer_spec(shape, layer):
    nd = len(shape)
    return pl.BlockSpec((None,) + tuple(shape[1:]), lambda *_: (layer,) + (0,) * (nd - 1),
                        pipeline_mode=pl.Buffered(1))


def _params(semantics):
    return pltpu.CompilerParams(dimension_semantics=semantics, vmem_limit_bytes=VMEM_LIMIT)


def _even_kernel(h_ref, halo_ref, g_ref, win_ref, vg_ref, ws_ref, bs_ref, pw_ref, ps_ref, *refs):
    n_cast = (len(refs) - 3) // 2
    cast_in, y_ref, cast_out = refs[:n_cast], refs[n_cast], refs[n_cast + 1:2 * n_cast + 1]
    xs_ref, pb_ref = refs[2 * n_cast + 1:]
    for src, dst in zip(cast_in, cast_out):
        dst[...] = src[...].astype(BF16)
    tm = h_ref.shape[1]
    i = pl.program_id(1)
    g = g_ref[...]
    xh = jnp.where(i == 0, 0.0, halo_ref[0])
    xs_ref[0:HALO, :] = _rms(xh, g).astype(BF16)
    xs_ref[HALO:, :] = _rms(h_ref[0], g).astype(BF16)

    z = _gelu(_mm(xs_ref[HALO:, :], win_ref[:, :2 * A_WIDTH]))
    u = z[:, :A_WIDTH]
    v = _rms(z[:, A_WIDTH:], vg_ref[...]).astype(BF16)
    n_chunks = tm // CHUNK
    row = lax.broadcasted_iota(jnp.int32, (CHUNK, CHUNK), 0)
    col = lax.broadcasted_iota(jnp.int32, (CHUNK, CHUNK), 1)
    for grp in range(A_GROUPS):
        lanes = slice(grp * LANES, (grp + 1) * LANES)
        w = jnp.where(row >= col, ws_ref[grp], 0.0).astype(BF16)
        rhs = jnp.concatenate([v[c * CHUNK:(c + 1) * CHUNK, lanes] for c in range(n_chunks)], axis=1)
        mixed = _mm(w, rhs) + bs_ref[:, grp:grp + 1]
        for c in range(n_chunks):
            rows = slice(c * CHUNK, (c + 1) * CHUNK)
            y_ref[0, rows, lanes] = (u[rows, lanes] * mixed[:, c * CHUNK:(c + 1) * CHUNK]).astype(BF16)

    pb_ref[...] = _mm(xs_ref[...], win_ref[:, 2 * A_WIDTH:])
    pos = i * tm + lax.broadcasted_iota(jnp.int32, (tm, 1), 0)
    for grp, window in enumerate(POOL_WINDOWS):
        lanes = slice(grp * LANES, (grp + 1) * LANES)
        acc = pb_ref[:, lanes]
        span = 1
        while span < window:
            acc = acc + pltpu.roll(acc, span, 0)
            span *= 2
        inv_count = 1.0 / jnp.minimum(pos + 1, window).astype(F32)
        pooled = (acc[HALO:] * inv_count - pb_ref[HALO:, lanes]).astype(BF16)
        yb = _mm(pooled, pw_ref[grp]) * ps_ref[:, lanes]
        y_ref[0, :, A_WIDTH + grp * LANES:A_WIDTH + (grp + 1) * LANES] = yb.astype(BF16)


def _even_mixer(h, gain, w_in, v_gain, w_s, b_s, pool_w, pool_scale, later_weights):
    bn, s, _ = h.shape
    tm = TOKEN_TILE
    n_steps = bn * (s // tm)
    main = pl.BlockSpec((1, tm, D_MODEL), lambda b, i: (b, i, 0))
    halo = pl.BlockSpec((1, HALO, D_MODEL), lambda b, i: (b, jnp.maximum(i * (tm // HALO) - 1, 0), 0))
    slabs = [pl.BlockSpec((w.shape[0] // n_steps, w.shape[1]), lambda b, i: (b * (s // tm) + i, 0))
             for w in later_weights]
    outs = pl.pallas_call(
        _even_kernel,
        grid=(bn, s // tm),
        in_specs=[main, halo, _const_spec((1, D_MODEL)), _const_spec(w_in.shape), _const_spec((1, A_WIDTH)),
                  _const_spec(w_s.shape), _const_spec(b_s.shape), _const_spec(pool_w.shape),
                  _const_spec((1, B_WIDTH))] + slabs,
        out_specs=[main] + slabs,
        out_shape=[jax.ShapeDtypeStruct(h.shape, BF16)] + [jax.ShapeDtypeStruct(w.shape, BF16) for w in later_weights],
        scratch_shapes=[pltpu.VMEM((tm + HALO, D_MODEL), BF16), pltpu.VMEM((tm + HALO, B_WIDTH), F32)],
        compiler_params=_params(("parallel", "parallel")),
        name="even_mixer",
    )(h, h, gain, w_in, v_gain, w_s, b_s, pool_w, pool_scale, *later_weights)
    return outs[0], outs[1:]


def _ffn_kernel(h_ref, ya_ref, yb_ref, wo_ref, g_ref, wup_ref, cw_ref, cb_ref, wdown_ref, fg_ref, o_ref,
                xs_ref, a_ref, hg_ref, *, final_norm):
    tm = h_ref.shape[1]
    wa = ya_ref.shape[2]
    i = pl.program_id(1)

    @pl.when(i == 0)
    def _():
        xs_ref[0:HALO, :] = jnp.zeros((HALO, D_MODEL), BF16)

    @pl.when(i > 0)
    def _():
        xs_ref[0:HALO, :] = xs_ref[tm:tm + HALO, :]

    hm = h_ref[0] + _mm(ya_ref[0], wo_ref[0:wa, :]) + _mm(yb_ref[0], wo_ref[wa:, :])
    xs_ref[HALO:, :] = _rms(hm, g_ref[...]).astype(BF16)
    for c in range(D_FF // FF_CHUNK):
        cols = slice(c * FF_CHUNK, (c + 1) * FF_CHUNK)
        a_ref[...] = _mm(xs_ref[...], wup_ref[:, cols])
        gate = _mm(xs_ref[HALO:, :], wup_ref[:, D_FF + c * FF_CHUNK:D_FF + (c + 1) * FF_CHUNK])
        conv = (a_ref[pl.ds(HALO - 2, tm), :] * cw_ref[0:1, cols]
                + a_ref[pl.ds(HALO - 1, tm), :] * cw_ref[1:2, cols]
                + a_ref[pl.ds(HALO, tm), :] * cw_ref[2:3, cols]
                + cb_ref[:, cols])
        hg_ref[:, cols] = (_gelu(conv) * gate).astype(BF16)
    y = hm + _mm(hg_ref[...], wdown_ref[...])
    if final_norm:
        y = _rms(y, fg_ref[...])
    o_ref[0] = y


def _mix_ffn(h, ya, yb, col_a, col_b, w_out, gains, w_up, conv_w, conv_b, w_down, final_gain, layer, final_norm):
    bn, s, _ = h.shape
    tm = TOKEN_TILE
    wa, wb = ya.shape[2] // (col_a[1]), yb.shape[2] // (col_b[1])
    main = pl.BlockSpec((1, tm, D_MODEL), lambda b, i: (b, i, 0))
    return pl.pallas_call(
        functools.partial(_ffn_kernel, final_norm=final_norm),
        grid=(bn, s // tm),
        in_specs=[main,
                  pl.BlockSpec((1, tm, wa), lambda b, i: (b, i, col_a[0])),
                  pl.BlockSpec((1, tm, wb), lambda b, i: (b, i, col_b[0])),
                  _const_spec(w_out.shape), _layer_spec(gains.shape, layer), _layer_spec(w_up.shape, layer),
                  _layer_spec(conv_w.shape, layer), _layer_spec(conv_b.shape, layer),
                  _layer_spec(w_down.shape, layer), _const_spec((1, D_MODEL))],
        out_specs=main,
        out_shape=jax.ShapeDtypeStruct(h.shape, F32),
        scratch_shapes=[pltpu.VMEM((tm + HALO, D_MODEL), BF16), pltpu.VMEM((tm + HALO, FF_CHUNK), F32),
                        pltpu.VMEM((tm, D_FF), BF16)],
        compiler_params=_params(("parallel", "arbitrary")),
        name="mix_ffn",
    )(h, ya, yb, w_out, gains, w_up, conv_w, conv_b, w_down, final_gain)


def _rope(x, tab):
    c, s_up, s_lo = tab
    return x * c + pltpu.roll(x, ROPE_DIM // 2, 1) * s_up + pltpu.roll(x, LANES - ROPE_DIM // 2, 1) * s_lo


def _qkv_kernel(h_ref, pos_ref, g_ref, inv_ref, wqt_ref, wvt_ref, wn_ref,
                cqt_ref, ck_ref, cvt_ref, dq0_ref, dk0_ref, dv0_ref, dq1_ref, dk1_ref, dv1_ref,
                dq2_ref, dk2_ref, dv2_ref, xs_ref, tab_ref, perm_ref):
    tm = h_ref.shape[1]
    half = ROPE_DIM // 2
    xs_ref[...] = _rms(h_ref[0], g_ref[...]).astype(BF16)

    ang = inv_ref[...] * pos_ref[0].astype(F32)
    cos_t, sin_t = jnp.cos(ang), jnp.sin(ang)
    rest = HEAD_DIM - ROPE_DIM
    ones, z_half, z_rest = jnp.ones((rest, tm), F32), jnp.zeros((half, tm), F32), jnp.zeros((rest, tm), F32)
    tab_ref[0] = jnp.concatenate([cos_t, cos_t, ones] * 2, axis=0).T
    tab_ref[1] = jnp.concatenate([z_half, sin_t, z_rest] * 2, axis=0).T
    tab_ref[2] = jnp.concatenate([-sin_t, z_half, z_rest] * 2, axis=0).T

    tabs = (tab_ref[0], tab_ref[1], tab_ref[2])

    def project(col0, width, rope, scale):
        outs = []
        for c in range(width // MXU_COLS):
            r = _mm(xs_ref[...], wn_ref[:, col0 + c * MXU_COLS:col0 + (c + 1) * MXU_COLS])
            for x in (r[:, :LANES], r[:, LANES:]):
                if rope:
                    x = _rope(x, tabs)
                if scale is not None:
                    x = x * scale
                outs.append(x)
        return outs

    dscale = HEAD_DIM ** -0.5 * math.log2(math.e)
    gw = D_GROUP_WIDTH
    slot = 0
    group_refs = ((dq0_ref, dk0_ref, dv0_ref), (dq1_ref, dk1_ref, dv1_ref), (dq2_ref, dk2_ref, dv2_ref))
    for grp in reversed(range(len(D_PATTERNS))):
        dil = D_PATTERNS[grp][1]
        n = tm // dil
        for part in (2, 0, 1):
            ref, rope, scale = group_refs[grp][part], part < 2, dscale if part == 0 else None
            col0 = 3 * C_WIDTH + part * len(D_PATTERNS) * gw + grp * gw
            for c, res in enumerate(project(col0, gw, rope, scale)):
                lanes = slice(c * LANES, (c + 1) * LANES)
                if dil == 1:
                    ref[0, 0, :, lanes] = res.astype(BF16)
                else:
                    perm_ref[slot] = res
                    for r in range(dil):
                        ref[0, r, :, lanes] = perm_ref[slot, pl.ds(r, n, stride=dil), :].astype(BF16)
                    slot += 1

    for c, r in enumerate(project(C_WIDTH, C_WIDTH, True, None)):
        ck_ref[0, :, c * LANES:(c + 1) * LANES] = r.astype(BF16)

    qt = _mm_nt(wqt_ref[...], xs_ref[...])
    pieces = []
    for hd in range(C_WIDTH // HEAD_DIM):
        x1 = qt[hd * HEAD_DIM:hd * HEAD_DIM + half]
        x2 = qt[hd * HEAD_DIM + half:hd * HEAD_DIM + ROPE_DIM]
        pieces += [x1 * cos_t - x2 * sin_t, x2 * cos_t + x1 * sin_t, qt[hd * HEAD_DIM + ROPE_DIM:(hd + 1) * HEAD_DIM]]
    qt = jnp.concatenate(pieces, axis=0) * (HEAD_DIM ** -0.5 * math.log2(math.e))
    for hp in range(C_HEADS):
        cqt_ref[0, hp, 0] = qt[hp * LANES:(hp + 1) * LANES].astype(BF16)

    vt = _mm_nt(wvt_ref[...], xs_ref[...])
    for hp in range(C_HEADS):
        for j in range(tm // ATT_K):
            cvt_ref[0, hp, j] = vt[hp * LANES:(hp + 1) * LANES, j * ATT_K:(j + 1) * ATT_K].astype(BF16)


def _qkv_proj(h, positions, gain, inv_col, wq_t, wv_t, w_nat):
    bn, s, _ = h.shape
    tm = TOKEN_TILE
    main = pl.BlockSpec((1, tm, D_MODEL), lambda b, i: (b, i, 0))
    pos_spec = pl.BlockSpec((1, 1, tm), lambda b, i: (b, 0, i))
    out_shapes = [jax.ShapeDtypeStruct((bn, C_HEADS, s // ATT_Q, LANES, ATT_Q), BF16),
                  jax.ShapeDtypeStruct((bn, s, C_WIDTH), BF16),
                  jax.ShapeDtypeStruct((bn, C_HEADS, s // ATT_K, LANES, ATT_K), BF16)]
    out_specs = [pl.BlockSpec((1, C_HEADS, tm // ATT_Q, LANES, ATT_Q), lambda b, i: (b, 0, i, 0, 0)),
                 pl.BlockSpec((1, tm, C_WIDTH), lambda b, i: (b, i, 0)),
                 pl.BlockSpec((1, C_HEADS, tm // ATT_K, LANES, ATT_K), lambda b, i: (b, 0, i, 0, 0))]
    n_perm = 0
    for _, dil in D_PATTERNS:
        out_shapes += [jax.ShapeDtypeStruct((bn, dil, s // dil, D_GROUP_WIDTH), BF16)] * 3
        out_specs += [pl.BlockSpec((1, dil, tm // dil, D_GROUP_WIDTH), lambda b, i: (b, 0, i, 0))] * 3
        n_perm += 0 if dil == 1 else 3 * D_GROUP_WIDTH // LANES
    return pl.pallas_call(
        _qkv_kernel,
        grid=(bn, s // tm),
        in_specs=[main, pos_spec, _const_spec((1, D_MODEL)), _const_spec(inv_col.shape), _const_spec(wq_t.shape),
                  _const_spec(wv_t.shape), _const_spec(w_nat.shape)],
        out_specs=out_specs,
        out_shape=out_shapes,
        scratch_shapes=[pltpu.VMEM((tm, D_MODEL), BF16), pltpu.VMEM((3, tm, LANES), F32),
                        pltpu.VMEM((n_perm, tm, LANES), F32)],
        compiler_params=_params(("parallel", "parallel")),
        name="qkv_proj",
    )(h, positions.reshape(bn, 1, s), gain, inv_col, wq_t, wv_t, w_nat)


def _diff_kernel(qt_ref, k_ref, vt_ref, lam_ref, sg_ref, o_ref, qc_ref, s_ref, e_ref, m_ref, acc_ref):
    tq, tk = ATT_Q, ATT_K
    n_strips = 2 * tq // LANES
    all_strips = tuple(range(n_strips))
    late = tuple(j for j in all_strips if (j * LANES) % tq >= tk)
    ones = jnp.ones((V_ROWS - LANES, tk), BF16)
    row = lax.broadcasted_iota(jnp.int32, (LANES, tq), 0)
    lam_rows = lam_ref[...]
    lam = (jnp.exp(jnp.sum(lam_rows[0:1] * lam_rows[1:2], axis=-1, keepdims=True))
           - jnp.exp(jnp.sum(lam_rows[2:3] * lam_rows[3:4], axis=-1, keepdims=True)) + LAMBDA_INIT)
    out_gain = sg_ref[...] * (1.0 - LAMBDA_INIT)

    def qk(kb, buf, strips=all_strips):
        start = pl.multiple_of(kb * tk, tk)
        rhs = qc_ref[...] if strips == all_strips else jnp.concatenate(
            [qc_ref[:, j * LANES:(j + 1) * LANES] for j in strips], axis=1)
        s = _mm(k_ref[0, pl.ds(start, tk), :], rhs)
        for idx, j in enumerate(strips):
            s_ref[buf, j] = s[:, idx * LANES:(idx + 1) * LANES]

    def soft(buf, kb, strips=all_strips, key_off=None):
        alphas = []
        for j in strips:
            cols = slice(j * LANES, (j + 1) * LANES)
            s = s_ref[buf, j]
            q_rel = (j * LANES) % tq
            if key_off is not None and q_rel < key_off + tk:
                key = key_off + lax.broadcasted_iota(jnp.int32, (tk, LANES), 0)
                qry = q_rel + lax.broadcasted_iota(jnp.int32, (tk, LANES), 1)
                s = jnp.where(key <= qry, s, NEG_BIG)
            m_old = m_ref[:, cols]
            m_new = jnp.maximum(m_old, jnp.max(s, axis=0, keepdims=True))
            alphas.append(jnp.exp2(m_old - m_new))
            m_ref[:, cols] = m_new
            e_ref[j] = jnp.exp2(s - m_new).astype(BF16)
        lhs = jnp.concatenate([vt_ref[0, 0, kb], ones], axis=0)
        pv = _mm(lhs, jnp.concatenate([e_ref[j] for j in strips], axis=1))
        if strips == all_strips:
            acc_ref[...] = acc_ref[...] * jnp.concatenate(alphas, axis=1) + pv
        else:
            for idx, j in enumerate(strips):
                cols = slice(j * LANES, (j + 1) * LANES)
                acc_ref[:, cols] = acc_ref[:, cols] * alphas[idx] + pv[:, idx * LANES:(idx + 1) * LANES]

    def load_queries(qj):
        qt = qt_ref[0, 0, qj]
        zero = jnp.zeros_like(qt)
        qc_ref[:, :tq] = jnp.where(row < HEAD_DIM, qt, zero)
        qc_ref[:, tq:] = jnp.where(row >= HEAD_DIM, qt, zero)

    n_blocks = qt_ref.shape[2]
    load_queries(0)
    qk(0, 0)

    def query_block(qj, _):
        m_ref[...] = jnp.full(m_ref.shape, NEG_BIG, F32)
        acc_ref[...] = jnp.zeros(acc_ref.shape, F32)

        def pair(it, _):
            qk(2 * it + 1, 1)
            soft(0, 2 * it)
            qk(2 * it + 2, 0)
            soft(1, 2 * it + 1)
            return 0

        lax.fori_loop(0, qj, pair, 0)
        qk(2 * qj + 1, 1, late)
        soft(0, 2 * qj, key_off=0)
        load_queries(jnp.minimum(qj + 1, n_blocks - 1))
        qk(0, 0)
        soft(1, 2 * qj + 1, late, key_off=tk)

        inv = 1.0 / acc_ref[LANES:LANES + 1, :]
        ot = acc_ref[0:LANES, :tq] * inv[:, :tq] - lam * (acc_ref[0:LANES, tq:] * inv[:, tq:])
        ot = ot * lax.rsqrt(jnp.mean(ot * ot, axis=0, keepdims=True) + EPS) * out_gain
        o_ref[0, pl.ds(pl.multiple_of(qj * tq, tq), tq), :] = ot.T.astype(BF16)
        return 0

    lax.fori_loop(0, n_blocks, query_block, 0)


def _diff_attention(cqt, ck, cvt, lam_rows, subln_gain_col):
    bn, s, _ = ck.shape
    tq, tk = ATT_Q, ATT_K
    return pl.pallas_call(
        _diff_kernel,
        grid=(bn, C_HEADS),
        in_specs=[pl.BlockSpec((1, 1, s // tq, LANES, tq), lambda b, h: (b, h, 0, 0, 0)),
                  pl.BlockSpec((1, s, LANES), lambda b, h: (b, 0, h)),
                  pl.BlockSpec((1, 1, s // tk, LANES, tk), lambda b, h: (b, h, 0, 0, 0)),
                  pl.BlockSpec(lam_rows.shape, lambda b, h: (0, 0)),
                  pl.BlockSpec((LANES, 1), lambda b, h: (0, 0))],
        out_specs=pl.BlockSpec((1, s, LANES), lambda b, h: (b, 0, h)),
        out_shape=jax.ShapeDtypeStruct((bn, s, C_WIDTH), BF16),
        scratch_shapes=[pltpu.VMEM((LANES, 2 * tq), BF16), pltpu.VMEM((2, 2 * tq // LANES, tk, LANES), F32),
                        pltpu.VMEM((2 * tq // LANES, tk, LANES), BF16), pltpu.VMEM((1, 2 * tq), F32),
                        pltpu.VMEM((V_ROWS, 2 * tq), F32)],
        compiler_params=_params(("parallel", "parallel")),
        name="diff_attention",
    )(cqt, ck, cvt, lam_rows, subln_gain_col)


def _dil_kernel(q0, k0, v0, q1, k1, v1, q2, k2, v2, o_ref, os_ref, ls_ref):
    nk = DIL_BLOCK
    s_len = o_ref.shape[1]
    lane = lax.broadcasted_iota(jnp.int32, (nk, LANES), 1)
    first_head = lane < HEAD_DIM
    row = lax.broadcasted_iota(jnp.int32, (2 * nk, 2 * nk), 0) % nk
    col = lax.broadcasted_iota(jnp.int32, (2 * nk, 2 * nk), 1)
    band = ((col < nk) & (col >= row)) | ((col >= nk) & (col - nk <= row))
    ones = jnp.ones((2 * nk, LANES), BF16)

    for grp, ((_, dil), q_ref, k_ref, v_ref) in enumerate(zip(D_PATTERNS, (q0, q1, q2), (k0, k1, k2), (v0, v1, v2))):
        nb = s_len // dil // nk
        per_res = min(nb, DIL_BATCH)
        n_res = DIL_BATCH // per_res
        chunks = nb // per_res

        def batch(t, _, dil=dil, per_res=per_res, n_res=n_res, chunks=chunks, grp=grp,
                  q_ref=q_ref, k_ref=k_ref, v_ref=v_ref):
            n0 = (t % chunks) * per_res
            base = pl.multiple_of(n0 * nk, nk)
            prev = pl.multiple_of(jnp.maximum(n0 - 1, 0) * nk, nk)
            for rr in range(n_res):
                r = (t // chunks) * n_res + rr
                q_all = q_ref[0, r, pl.ds(base, per_res * nk), :]
                k_all = jnp.concatenate([k_ref[0, r, pl.ds(prev, nk), :], k_ref[0, r, pl.ds(base, per_res * nk), :]], axis=0)
                v_all = jnp.concatenate([v_ref[0, r, pl.ds(prev, nk), :], v_ref[0, r, pl.ds(base, per_res * nk), :]], axis=0)
                for i in range(per_res):
                    q = q_all[i * nk:(i + 1) * nk]
                    k = k_all[i * nk:(i + 2) * nk]
                    v = jnp.concatenate([v_all[i * nk:(i + 2) * nk], ones], axis=1)
                    mask = band
                    if i == 0:
                        mask = band & (col >= jnp.where(n0 == 0, nk, 0))
                    zero = jnp.zeros_like(q)
                    q2 = jnp.concatenate([jnp.where(first_head, q, zero), jnp.where(first_head, zero, q)], axis=0)
                    s = jnp.where(mask, _mm_nt(q2, k), NEG_BIG)
                    m = jnp.max(s, axis=-1, keepdims=True)
                    pv = _mm(jnp.exp2(s - m).astype(BF16), v)
                    num = jnp.where(first_head, pv[:nk, :LANES], pv[nk:, :LANES])
                    den = jnp.where(first_head, pv[:nk, LANES:], pv[nk:, LANES:])
                    top = jnp.where(first_head, jnp.broadcast_to(m[:nk], (nk, LANES)), jnp.broadcast_to(m[nk:], (nk, LANES)))
                    rows = pl.ds((n0 + i) * (nk * dil) + r, nk, stride=dil)
                    os_ref[grp, rows, :] = num / den
                    ls_ref[grp, rows, :] = top + jnp.log2(den)
            return 0

        lax.fori_loop(0, (dil // n_res) * chunks, batch, 0)

    tile = 512
    for t in range(s_len // tile):
        rows = slice(t * tile, (t + 1) * tile)
        lse = [ls_ref[grp, rows, :] for grp in range(3)]
        top = jnp.maximum(jnp.maximum(lse[0], lse[1]), lse[2])
        wts = [jnp.exp2(x - top) for x in lse]
        num = wts[0] * os_ref[0, rows, :] + wts[1] * os_ref[1, rows, :] + wts[2] * os_ref[2, rows, :]
        o_ref[0, rows, :] = (num / (wts[0] + wts[1] + wts[2])).astype(BF16)


def _dilated_attention(dqkv):
    bn = dqkv[0].shape[0]
    s = dqkv[0].shape[2]
    specs = []
    for (_, dil) in D_PATTERNS:
        specs += [pl.BlockSpec((1, dil, s // dil, LANES), lambda b, hp: (b, 0, 0, hp))] * 3
    return pl.pallas_call(
        _dil_kernel,
        grid=(bn, D_GROUP_WIDTH // LANES),
        in_specs=specs,
        out_specs=pl.BlockSpec((1, s, LANES), lambda b, hp: (b, 0, hp)),
        out_shape=jax.ShapeDtypeStruct((bn, s, D_GROUP_WIDTH), BF16),
        scratch_shapes=[pltpu.VMEM((3, s, LANES), F32), pltpu.VMEM((3, s, LANES), F32)],
        compiler_params=_params(("parallel", "parallel")),
        name="dilated_attention",
    )(*dqkv)


def kernel(x, positions, norm_mix, norm_ffn, final_norm, even_w_in, gmlp_v_gain, gmlp_w_s, gmlp_b_s, pool_w,
           pool_scale, even_w_out, odd_w_in, lambda_q1, lambda_k1, lambda_q2, lambda_k2, subln_gain, odd_w_out,
           ffn_w_up, ffn_conv_w, ffn_conv_b, ffn_w_down):
    row = lambda a: a.reshape(1, -1)
    bf = lambda a: a.astype(BF16)
    depth = norm_ffn.shape[0]

    later = [ffn_w_up.reshape(depth * D_MODEL, 2 * D_FF), ffn_w_down.reshape(depth * D_FF, D_MODEL),
             odd_w_in[0], odd_w_out[0], even_w_out[0]]
    y, (w_up, w_down, w_in, w_out_odd, w_out_even) = _even_mixer(
        x, row(norm_mix[0]), bf(even_w_in[0]), row(gmlp_v_gain[0]), gmlp_w_s[0], gmlp_b_s[0].T,
        bf(pool_w[0]), row(pool_scale[0]), later)
    ffn_params = (norm_ffn.reshape(depth, 1, D_MODEL), w_up.reshape(depth, D_MODEL, 2 * D_FF),
                  ffn_conv_w.reshape(depth, 3, D_FF), ffn_conv_b.reshape(depth, 1, D_FF),
                  w_down.reshape(depth, D_FF, D_MODEL), row(final_norm))
    h = _mix_ffn(x, y, y, (0, 2), (1, 2), w_out_even, *ffn_params, layer=0, final_norm=False)

    inv = ROPE_THETA ** (-jnp.arange(0, ROPE_DIM, 2, dtype=F32) / ROPE_DIM)
    qkv = _qkv_proj(h, positions, row(norm_mix[1]), inv.reshape(-1, 1), w_in[:, :C_WIDTH].T,
                    w_in[:, 2 * C_WIDTH:3 * C_WIDTH].T, w_in)
    lam_rows = jnp.stack([lambda_q1[0], lambda_k1[0], lambda_q2[0], lambda_k2[0]])
    yc = _diff_attention(qkv[0], qkv[1], qkv[2], lam_rows, subln_gain[0].reshape(-1, 1))
    yd = _dilated_attention(qkv[3:])
    return _mix_ffn(h, yc, yd, (0, 1), (0, 1), w_out_odd, *ffn_params, layer=1, final_norm=True)
```

```python
import functools
import math

import jax
import jax.numpy as jnp
from jax import lax
from jax.experimental import pallas as pl
from jax.experimental.pallas import tpu as pltpu

F32 = jnp.float32
BF16 = jnp.bfloat16

D_MODEL = 1024
A_WIDTH = 512
A_GROUPS = 4
CHUNK = 128
B_WIDTH = 512
POOL_WINDOWS = (2, 4, 8, 16)
HEAD_DIM = 64
C_HEADS = 6
C_WIDTH = 768
D_PATTERNS = ((128, 1), (512, 4), (2048, 16))
D_GROUP_WIDTH = 256
ROPE_THETA = 500000.0
ROPE_DIM = 16
D_FF = 2816
EPS = 1e-6
LAMBDA_INIT = 0.8 - 0.6 * math.exp(-0.3 * 1)

LANES = 128
HALO = 16
TOKEN_TILE = 512
FF_CHUNK = 256
ATT_Q = 512
ATT_K = 256
MXU_COLS = 256
ATT_STREAMS = 3
V_ROWS = 144
DIL_BATCH = 16
DIL_BLOCK = 128
VMEM_LIMIT = 56 * 1024 * 1024
NEG_BIG = -1e30


def _rms(x, g):
    return x * lax.rsqrt(jnp.mean(x * x, axis=-1, keepdims=True) + EPS) * g


def _gelu(x):
    c = math.sqrt(2.0 / math.pi)
    return 0.5 * x * (1.0 + jnp.tanh(c * (x + 0.044715 * (x * x * x))))


def _mm(a, b):
    return jnp.dot(a, b, preferred_element_type=F32)


def _mm_nt(a, b):
    return lax.dot_general(a, b, (((1,), (1,)), ((), ())), preferred_element_type=F32)


def _const_spec(shape):
    nd = len(shape)
    return pl.BlockSpec(shape, lambda *_: (0,) * nd, pipeline_mode=pl.Buffered(1))


def _layer_spec(shape, layer):
    nd = len(shape)
    return pl.BlockSpec((None,) + tuple(shape[1:]), lambda *_: (layer,) + (0,) * (nd - 1),
                        pipeline_mode=pl.Buffered(1))


def _params(semantics):
    return pltpu.CompilerParams(dimension_semantics=semantics, vmem_limit_bytes=VMEM_LIMIT)


def _even_kernel(h_ref, halo_ref, g_ref, win_ref, vg_ref, ws_ref, bs_ref, pw_ref, ps_ref, *refs):
    n_cast = (len(refs) - 3) // 2
    cast_in, y_ref, cast_out = refs[:n_cast], refs[n_cast], refs[n_cast + 1:2 * n_cast + 1]
    xs_ref, pb_ref = refs[2 * n_cast + 1:]
    for src, dst in zip(cast_in, cast_out):
        dst[...] = src[...].astype(BF16)
    tm = h_ref.shape[1]
    i = pl.program_id(1)
    g = g_ref[...]
    xh = jnp.where(i == 0, 0.0, halo_ref[0])
    xs_ref[0:HALO, :] = _rms(xh, g).astype(BF16)
    xs_ref[HALO:, :] = _rms(h_ref[0], g).astype(BF16)

    z = _gelu(_mm(xs_ref[HALO:, :], win_ref[:, :2 * A_WIDTH]))
    u = z[:, :A_WIDTH]
    v = _rms(z[:, A_WIDTH:], vg_ref[...]).astype(BF16)
    n_chunks = tm // CHUNK
    row = lax.broadcasted_iota(jnp.int32, (CHUNK, CHUNK), 0)
    col = lax.broadcasted_iota(jnp.int32, (CHUNK, CHUNK), 1)
    for grp in range(A_GROUPS):
        lanes = slice(grp * LANES, (grp + 1) * LANES)
        w = jnp.where(row >= col, ws_ref[grp], 0.0).astype(BF16)
        rhs = jnp.concatenate([v[c * CHUNK:(c + 1) * CHUNK, lanes] for c in range(n_chunks)], axis=1)
        mixed = _mm(w, rhs) + bs_ref[:, grp:grp + 1]
        for c in range(n_chunks):
            rows = slice(c * CHUNK, (c + 1) * CHUNK)
            y_ref[0, rows, lanes] = (u[rows, lanes] * mixed[:, c * CHUNK:(c + 1) * CHUNK]).astype(BF16)

    pb_ref[...] = _mm(xs_ref[...], win_ref[:, 2 * A_WIDTH:])
    pos = i * tm + lax.broadcasted_iota(jnp.int32, (tm, 1), 0)
    for grp, window in enumerate(POOL_WINDOWS):
        lanes = slice(grp * LANES, (grp + 1) * LANES)
        acc = pb_ref[:, lanes]
        span = 1
        while span < window:
            acc = acc + pltpu.roll(acc, span, 0)
            span *= 2
        inv_count = 1.0 / jnp.minimum(pos + 1, window).astype(F32)
        pooled = (acc[HALO:] * inv_count - pb_ref[HALO:, lanes]).astype(BF16)
        yb = _mm(pooled, pw_ref[grp]) * ps_ref[:, lanes]
        y_ref[0, :, A_WIDTH + grp * LANES:A_WIDTH + (grp + 1) * LANES] = yb.astype(BF16)


def _even_mixer(h, gain, w_in, v_gain, w_s, b_s, pool_w, pool_scale, later_weights):
    bn, s, _ = h.shape
    tm = TOKEN_TILE
    n_steps = bn * (s // tm)
    main = pl.BlockSpec((1, tm, D_MODEL), lambda b, i: (b, i, 0))
    halo = pl.BlockSpec((1, HALO, D_MODEL), lambda b, i: (b, jnp.maximum(i * (tm // HALO) - 1, 0), 0))
    slabs = [pl.BlockSpec((w.shape[0] // n_steps, w.shape[1]), lambda b, i: (b * (s // tm) + i, 0))
             for w in later_weights]
    outs = pl.pallas_call(
        _even_kernel,
        grid=(bn, s // tm),
        in_specs=[main, halo, _const_spec((1, D_MODEL)), _const_spec(w_in.shape), _const_spec((1, A_WIDTH)),
                  _const_spec(w_s.shape), _const_spec(b_s.shape), _const_spec(pool_w.shape),
                  _const_spec((1, B_WIDTH))] + slabs,
        out_specs=[main] + slabs,
        out_shape=[jax.ShapeDtypeStruct(h.shape, BF16)] + [jax.ShapeDtypeStruct(w.shape, BF16) for w in later_weights],
        scratch_shapes=[pltpu.VMEM((tm + HALO, D_MODEL), BF16), pltpu.VMEM((tm + HALO, B_WIDTH), F32)],
        compiler_params=_params(("parallel", "parallel")),
        name="even_mixer",
    )(h, h, gain, w_in, v_gain, w_s, b_s, pool_w, pool_scale, *later_weights)
    return outs[0], outs[1:]


def _ffn_kernel(h_ref, ya_ref, yb_ref, wo_ref, g_ref, wup_ref, cw_ref, cb_ref, wdown_ref, fg_ref, o_ref,
                xs_ref, a_ref, hg_ref, *, final_norm):
    tm = h_ref.shape[1]
    wa = ya_ref.shape[2]
    i = pl.program_id(1)

    @pl.when(i == 0)
    def _():
        xs_ref[0:HALO, :] = jnp.zeros((HALO, D_MODEL), BF16)

    @pl.when(i > 0)
    def _():
        xs_ref[0:HALO, :] = xs_ref[tm:tm + HALO, :]

    hm = h_ref[0] + _mm(ya_ref[0], wo_ref[0:wa, :]) + _mm(yb_ref[0], wo_ref[wa:, :])
    xs_ref[HALO:, :] = _rms(hm, g_ref[...]).astype(BF16)
    for c in range(D_FF // FF_CHUNK):
        cols = slice(c * FF_CHUNK, (c + 1) * FF_CHUNK)
        a_ref[...] = _mm(xs_ref[...], wup_ref[:, cols])
        gate = _mm(xs_ref[HALO:, :], wup_ref[:, D_FF + c * FF_CHUNK:D_FF + (c + 1) * FF_CHUNK])
        conv = (a_ref[pl.ds(HALO - 2, tm), :] * cw_ref[0:1, cols]
                + a_ref[pl.ds(HALO - 1, tm), :] * cw_ref[1:2, cols]
                + a_ref[pl.ds(HALO, tm), :] * cw_ref[2:3, cols]
                + cb_ref[:, cols])
        hg_ref[:, cols] = (_gelu(conv) * gate).astype(BF16)
    y = hm + _mm(hg_ref[...], wdown_ref[...])
    if final_norm:
        y = _rms(y, fg_ref[...])
    o_ref[0] = y


def _mix_ffn(h, ya, yb, col_a, col_b, w_out, gains, w_up, conv_w, conv_b, w_down, final_gain, layer, final_norm):
    bn, s, _ = h.shape
    tm = TOKEN_TILE
    wa, wb = ya.shape[2] // (col_a[1]), yb.shape[2] // (col_b[1])
    main = pl.BlockSpec((1, tm, D_MODEL), lambda b, i: (b, i, 0))
    return pl.pallas_call(
        functools.partial(_ffn_kernel, final_norm=final_norm),
        grid=(bn, s // tm),
        in_specs=[main,
                  pl.BlockSpec((1, tm, wa), lambda b, i: (b, i, col_a[0])),
                  pl.BlockSpec((1, tm, wb), lambda b, i: (b, i, col_b[0])),
                  _const_spec(w_out.shape), _layer_spec(gains.shape, layer), _layer_spec(w_up.shape, layer),
                  _layer_spec(conv_w.shape, layer), _layer_spec(conv_b.shape, layer),
                  _layer_spec(w_down.shape, layer), _const_spec((1, D_MODEL))],
        out_specs=main,
        out_shape=jax.ShapeDtypeStruct(h.shape, F32),
        scratch_shapes=[pltpu.VMEM((tm + HALO, D_MODEL), BF16), pltpu.VMEM((tm + HALO, FF_CHUNK), F32),
                        pltpu.VMEM((tm, D_FF), BF16)],
        compiler_params=_params(("parallel", "arbitrary")),
        name="mix_ffn",
    )(h, ya, yb, w_out, gains, w_up, conv_w, conv_b, w_down, final_gain)


def _rope(x, tab):
    c, s_up, s_lo = tab
    return x * c + pltpu.roll(x, ROPE_DIM // 2, 1) * s_up + pltpu.roll(x, LANES - ROPE_DIM // 2, 1) * s_lo


def _qkv_kernel(h_ref, pos_ref, g_ref, inv_ref, wqt_ref, wvt_ref, wn_ref,
                cqt_ref, ck_ref, cvt_ref, dq0_ref, dk0_ref, dv0_ref, dq1_ref, dk1_ref, dv1_ref,
                dq2_ref, dk2_ref, dv2_ref, xs_ref, tab_ref, perm_ref):
    tm = h_ref.shape[1]
    half = ROPE_DIM // 2
    xs_ref[...] = _rms(h_ref[0], g_ref[...]).astype(BF16)

    ang = inv_ref[...] * pos_ref[0].astype(F32)
    cos_t, sin_t = jnp.cos(ang), jnp.sin(ang)
    rest = HEAD_DIM - ROPE_DIM
    ones, z_half, z_rest = jnp.ones((rest, tm), F32), jnp.zeros((half, tm), F32), jnp.zeros((rest, tm), F32)
    tab_ref[0] = jnp.concatenate([cos_t, cos_t, ones] * 2, axis=0).T
    tab_ref[1] = jnp.concatenate([z_half, sin_t, z_rest] * 2, axis=0).T
    tab_ref[2] = jnp.concatenate([-sin_t, z_half, z_rest] * 2, axis=0).T

    tabs = (tab_ref[0], tab_ref[1], tab_ref[2])

    def project(col0, width, rope, scale):
        outs = []
        for c in range(width // MXU_COLS):
            r = _mm(xs_ref[...], wn_ref[:, col0 + c * MXU_COLS:col0 + (c + 1) * MXU_COLS])
            for x in (r[:, :LANES], r[:, LANES:]):
                if rope:
                    x = _rope(x, tabs)
                if scale is not None:
                    x = x * scale
                outs.append(x)
        return outs

    dscale = HEAD_DIM ** -0.5 * math.log2(math.e)
    gw = D_GROUP_WIDTH
    slot = 0
    group_refs = ((dq0_ref, dk0_ref, dv0_ref), (dq1_ref, dk1_ref, dv1_ref), (dq2_ref, dk2_ref, dv2_ref))
    for grp in reversed(range(len(D_PATTERNS))):
        dil = D_PATTERNS[grp][1]
        n = tm // dil
        for part in (2, 0, 1):
            ref, rope, scale = group_refs[grp][part], part < 2, dscale if part == 0 else None
            col0 = 3 * C_WIDTH + part * len(D_PATTERNS) * gw + grp * gw
            for c, res in enumerate(project(col0, gw, rope, scale)):
                lanes = slice(c * LANES, (c + 1) * LANES)
                if dil == 1:
                    ref[0, 0, :, lanes] = res.astype(BF16)
                else:
                    perm_ref[slot] = res
                    for r in range(dil):
                        ref[0, r, :, lanes] = perm_ref[slot, pl.ds(r, n, stride=dil), :].astype(BF16)
                    slot += 1

    for c, r in enumerate(project(C_WIDTH, C_WIDTH, True, None)):
        ck_ref[0, :, c * LANES:(c + 1) * LANES] = r.astype(BF16)

    qt = _mm_nt(wqt_ref[...], xs_ref[...])
    pieces = []
    for hd in range(C_WIDTH // HEAD_DIM):
        x1 = qt[hd * HEAD_DIM:hd * HEAD_DIM + half]
        x2 = qt[hd * HEAD_DIM + half:hd * HEAD_DIM + ROPE_DIM]
        pieces += [x1 * cos_t - x2 * sin_t, x2 * cos_t + x1 * sin_t, qt[hd * HEAD_DIM + ROPE_DIM:(hd + 1) * HEAD_DIM]]
    qt = jnp.concatenate(pieces, axis=0) * (HEAD_DIM ** -0.5 * math.log2(math.e))
    for hp in range(C_HEADS):
        cqt_ref[0, hp, 0] = qt[hp * LANES:(hp + 1) * LANES].astype(BF16)

    vt = _mm_nt(wvt_ref[...], xs_ref[...])
    for hp in range(C_HEADS):
        for j in range(tm // ATT_K):
            cvt_ref[0, hp, j] = vt[hp * LANES:(hp + 1) * LANES, j * ATT_K:(j + 1) * ATT_K].astype(BF16)


def _qkv_proj(h, positions, gain, inv_col, wq_t, wv_t, w_nat):
    bn, s, _ = h.shape
    tm = TOKEN_TILE
    main = pl.BlockSpec((1, tm, D_MODEL), lambda b, i: (b, i, 0))
    pos_spec = pl.BlockSpec((1, 1, tm), lambda b, i: (b, 0, i))
    out_shapes = [jax.ShapeDtypeStruct((bn, C_HEADS, s // ATT_Q, LANES, ATT_Q), BF16),
                  jax.ShapeDtypeStruct((bn, s, C_WIDTH), BF16),
                  jax.ShapeDtypeStruct((bn, C_HEADS, s // ATT_K, LANES, ATT_K), BF16)]
    out_specs = [pl.BlockSpec((1, C_HEADS, tm // ATT_Q, LANES, ATT_Q), lambda b, i: (b, 0, i, 0, 0)),
                 pl.BlockSpec((1, tm, C_WIDTH), lambda b, i: (b, i, 0)),
                 pl.BlockSpec((1, C_HEADS, tm // ATT_K, LANES, ATT_K), lambda b, i: (b, 0, i, 0, 0))]
    n_perm = 0
    for _, dil in D_PATTERNS:
        out_shapes += [jax.ShapeDtypeStruct((bn, dil, s // dil, D_GROUP_WIDTH), BF16)] * 3
        out_specs += [pl.BlockSpec((1, dil, tm // dil, D_GROUP_WIDTH), lambda b, i: (b, 0, i, 0))] * 3
        n_perm += 0 if dil == 1 else 3 * D_GROUP_WIDTH // LANES
    return pl.pallas_call(
        _qkv_kernel,
        grid=(bn, s // tm),
        in_specs=[main, pos_spec, _const_spec((1, D_MODEL)), _const_spec(inv_col.shape), _const_spec(wq_t.shape),
                  _const_spec(wv_t.shape), _const_spec(w_nat.shape)],
        out_specs=out_specs,
        out_shape=out_shapes,
        scratch_shapes=[pltpu.VMEM((tm, D_MODEL), BF16), pltpu.VMEM((3, tm, LANES), F32),
                        pltpu.VMEM((n_perm, tm, LANES), F32)],
        compiler_params=_params(("parallel", "parallel")),
        name="qkv_proj",
    )(h, positions.reshape(bn, 1, s), gain, inv_col, wq_t, wv_t, w_nat)


def _diff_kernel(qt_ref, k_ref, vt_ref, lam_ref, sg_ref, o_ref, qc_ref, s_ref, cm_ref, e_ref, m_ref, acc_ref):
    tq, tk = ATT_Q, ATT_K
    streams = range(qt_ref.shape[1])
    n_strips = 2 * tq // LANES
    all_strips = tuple(range(n_strips))
    late = tuple(j for j in all_strips if (j * LANES) % tq >= tk)
    ones = jnp.ones((V_ROWS - LANES, tk), BF16)
    row = lax.broadcasted_iota(jnp.int32, (LANES, tq), 0)
    lam_rows = lam_ref[...]
    lam = (jnp.exp(jnp.sum(lam_rows[0:1] * lam_rows[1:2], axis=-1, keepdims=True))
           - jnp.exp(jnp.sum(lam_rows[2:3] * lam_rows[3:4], axis=-1, keepdims=True)) + LAMBDA_INIT)
    out_gain = sg_ref[...] * (1.0 - LAMBDA_INIT)

    def qk(st, kb, buf, strips=all_strips):
        start = pl.multiple_of(kb * tk, tk)
        rhs = qc_ref[st] if strips == all_strips else jnp.concatenate(
            [qc_ref[st, :, j * LANES:(j + 1) * LANES] for j in strips], axis=1)
        s = _mm(k_ref[0, pl.ds(start, tk), st * LANES:(st + 1) * LANES], rhs)
        for idx, j in enumerate(strips):
            strip = s[:, idx * LANES:(idx + 1) * LANES]
            s_ref[st, buf, j] = strip
            cm_ref[st, buf, :, j * LANES:(j + 1) * LANES] = jnp.max(strip, axis=0, keepdims=True)

    def soft(st, buf, kb, strips=all_strips, key_off=None):
        alphas = []
        for j in strips:
            cols = slice(j * LANES, (j + 1) * LANES)
            s = s_ref[st, buf, j]
            q_rel = (j * LANES) % tq
            col_max = cm_ref[st, buf, :, cols]
            if key_off is not None and q_rel < key_off + tk:
                key = key_off + lax.broadcasted_iota(jnp.int32, (tk, LANES), 0)
                qry = q_rel + lax.broadcasted_iota(jnp.int32, (tk, LANES), 1)
                s = jnp.where(key <= qry, s, NEG_BIG)
                col_max = jnp.max(s, axis=0, keepdims=True)
            m_old = m_ref[st, :, cols]
            m_new = jnp.maximum(m_old, col_max)
            alphas.append(jnp.exp2(m_old - m_new))
            m_ref[st, :, cols] = m_new
            e_ref[st, j] = jnp.exp2(s - m_new).astype(BF16)
        lhs = jnp.concatenate([vt_ref[0, st, kb], ones], axis=0)
        pv = _mm(lhs, jnp.concatenate([e_ref[st, j] for j in strips], axis=1))
        if strips == all_strips:
            acc_ref[st] = acc_ref[st] * jnp.concatenate(alphas, axis=1) + pv
        else:
            for idx, j in enumerate(strips):
                cols = slice(j * LANES, (j + 1) * LANES)
                acc_ref[st, :, cols] = acc_ref[st, :, cols] * alphas[idx] + pv[:, idx * LANES:(idx + 1) * LANES]

    def load_queries(st, qj):
        qt = qt_ref[0, st, qj]
        zero = jnp.zeros_like(qt)
        qc_ref[st, :, :tq] = jnp.where(row < HEAD_DIM, qt, zero)
        qc_ref[st, :, tq:] = jnp.where(row >= HEAD_DIM, qt, zero)

    def finish(st, qj):
        inv = 1.0 / acc_ref[st, LANES:LANES + 1, :]
        ot = acc_ref[st, 0:LANES, :tq] * inv[:, :tq] - lam * (acc_ref[st, 0:LANES, tq:] * inv[:, tq:])
        ot = ot * lax.rsqrt(jnp.mean(ot * ot, axis=0, keepdims=True) + EPS) * out_gain
        o_ref[0, pl.ds(pl.multiple_of(qj * tq, tq), tq), st * LANES:(st + 1) * LANES] = ot.T.astype(BF16)

    n_blocks = qt_ref.shape[2]
    for st in streams:
        load_queries(st, 0)
        qk(st, 0, 0)

    def query_block(qj, _):
        m_ref[...] = jnp.full(m_ref.shape, NEG_BIG, F32)
        acc_ref[...] = jnp.zeros(acc_ref.shape, F32)

        def pair(it, _):
            for st in streams:
                qk(st, 2 * it + 1, 1)
                soft(st, 0, 2 * it)
                qk(st, 2 * it + 2, 0)
                soft(st, 1, 2 * it + 1)
            return 0

        lax.fori_loop(0, qj, pair, 0)
        for st in streams:
            qk(st, 2 * qj + 1, 1, late)
            soft(st, 0, 2 * qj, key_off=0)
            load_queries(st, jnp.minimum(qj + 1, n_blocks - 1))
            qk(st, 0, 0)
            soft(st, 1, 2 * qj + 1, late, key_off=tk)
            finish(st, qj)
        return 0

    lax.fori_loop(0, n_blocks, query_block, 0)


def _diff_attention(cqt, ck, cvt, lam_rows, subln_gain_col):
    bn, s, _ = ck.shape
    tq, tk, ns = ATT_Q, ATT_K, ATT_STREAMS
    return pl.pallas_call(
        _diff_kernel,
        grid=(bn, C_HEADS // ns),
        in_specs=[pl.BlockSpec((1, ns, s // tq, LANES, tq), lambda b, h: (b, h, 0, 0, 0)),
                  pl.BlockSpec((1, s, ns * LANES), lambda b, h: (b, 0, h)),
                  pl.BlockSpec((1, ns, s // tk, LANES, tk), lambda b, h: (b, h, 0, 0, 0)),
                  pl.BlockSpec(lam_rows.shape, lambda b, h: (0, 0)),
                  pl.BlockSpec((LANES, 1), lambda b, h: (0, 0))],
        out_specs=pl.BlockSpec((1, s, ns * LANES), lambda b, h: (b, 0, h)),
        out_shape=jax.ShapeDtypeStruct((bn, s, C_WIDTH), BF16),
        scratch_shapes=[pltpu.VMEM((ns, LANES, 2 * tq), BF16), pltpu.VMEM((ns, 2, 2 * tq // LANES, tk, LANES), F32),
                        pltpu.VMEM((ns, 2, 1, 2 * tq), F32),
                        pltpu.VMEM((ns, 2 * tq // LANES, tk, LANES), BF16), pltpu.VMEM((ns, 1, 2 * tq), F32),
                        pltpu.VMEM((ns, V_ROWS, 2 * tq), F32)],
        compiler_params=_params(("parallel", "parallel")),
        name="diff_attention",
    )(cqt, ck, cvt, lam_rows, subln_gain_col)


def _dil_kernel(q0, k0, v0, q1, k1, v1, q2, k2, v2, o_ref, os_ref, ls_ref):
    nk = DIL_BLOCK
    s_len = o_ref.shape[1]
    lane = lax.broadcasted_iota(jnp.int32, (nk, LANES), 1)
    first_head = lane < HEAD_DIM
    row = lax.broadcasted_iota(jnp.int32, (2 * nk, 2 * nk), 0) % nk
    col = lax.broadcasted_iota(jnp.int32, (2 * nk, 2 * nk), 1)
    band = ((col < nk) & (col >= row)) | ((col >= nk) & (col - nk <= row))
    ones = jnp.ones((2 * nk, LANES), BF16)

    for grp, ((_, dil), q_ref, k_ref, v_ref) in enumerate(zip(D_PATTERNS, (q0, q1, q2), (k0, k1, k2), (v0, v1, v2))):
        nb = s_len // dil // nk
        per_res = min(nb, DIL_BATCH)
        n_res = DIL_BATCH // per_res
        chunks = nb // per_res

        def batch(t, _, dil=dil, per_res=per_res, n_res=n_res, chunks=chunks, grp=grp,
                  q_ref=q_ref, k_ref=k_ref, v_ref=v_ref):
            n0 = (t % chunks) * per_res
            base = pl.multiple_of(n0 * nk, nk)
            prev = pl.multiple_of(jnp.maximum(n0 - 1, 0) * nk, nk)
            for rr in range(n_res):
                r = (t // chunks) * n_res + rr
                q_all = q_ref[0, r, pl.ds(base, per_res * nk), :]
                k_all = jnp.concatenate([k_ref[0, r, pl.ds(prev, nk), :], k_ref[0, r, pl.ds(base, per_res * nk), :]], axis=0)
                v_all = jnp.concatenate([v_ref[0, r, pl.ds(prev, nk), :], v_ref[0, r, pl.ds(base, per_res * nk), :]], axis=0)
                for i in range(per_res):
                    q = q_all[i * nk:(i + 1) * nk]
                    k = k_all[i * nk:(i + 2) * nk]
                    v = jnp.concatenate([v_all[i * nk:(i + 2) * nk], ones], axis=1)
                    mask = band
                    if i == 0:
                        mask = band & (col >= jnp.where(n0 == 0, nk, 0))
                    zero = jnp.zeros_like(q)
                    q2 = jnp.concatenate([jnp.where(first_head, q, zero), jnp.where(first_head, zero, q)], axis=0)
                    s = jnp.where(mask, _mm_nt(q2, k), NEG_BIG)
                    m = jnp.max(s, axis=-1, keepdims=True)
                    pv = _mm(jnp.exp2(s - m).astype(BF16), v)
                    num = jnp.where(first_head, pv[:nk, :LANES], pv[nk:, :LANES])
                    den = jnp.where(first_head, pv[:nk, LANES:], pv[nk:, LANES:])
                    top = jnp.where(first_head, jnp.broadcast_to(m[:nk], (nk, LANES)), jnp.broadcast_to(m[nk:], (nk, LANES)))
                    rows = pl.ds((n0 + i) * (nk * dil) + r, nk, stride=dil)
                    os_ref[grp, rows, :] = num / den
                    ls_ref[grp, rows, :] = top + jnp.log2(den)
            return 0

        lax.fori_loop(0, (dil // n_res) * chunks, batch, 0)

    tile = 512
    for t in range(s_len // tile):
        rows = slice(t * tile, (t + 1) * tile)
        lse = [ls_ref[grp, rows, :] for grp in range(3)]
        top = jnp.maximum(jnp.maximum(lse[0], lse[1]), lse[2])
        wts = [jnp.exp2(x - top) for x in lse]
        num = wts[0] * os_ref[0, rows, :] + wts[1] * os_ref[1, rows, :] + wts[2] * os_ref[2, rows, :]
        o_ref[0, rows, :] = (num / (wts[0] + wts[1] + wts[2])).astype(BF16)


def _dilated_attention(dqkv):
    bn = dqkv[0].shape[0]
    s = dqkv[0].shape[2]
    specs = []
    for (_, dil) in D_PATTERNS:
        specs += [pl.BlockSpec((1, dil, s // dil, LANES), lambda b, hp: (b, 0, 0, hp))] * 3
    return pl.pallas_call(
        _dil_kernel,
        grid=(bn, D_GROUP_WIDTH // LANES),
        in_specs=specs,
        out_specs=pl.BlockSpec((1, s, LANES), lambda b, hp: (b, 0, hp)),
        out_shape=jax.ShapeDtypeStruct((bn, s, D_GROUP_WIDTH), BF16),
        scratch_shapes=[pltpu.VMEM((3, s, LANES), F32), pltpu.VMEM((3, s, LANES), F32)],
        compiler_params=_params(("parallel", "parallel")),
        name="dilated_attention",
    )(*dqkv)


def kernel(x, positions, norm_mix, norm_ffn, final_norm, even_w_in, gmlp_v_gain, gmlp_w_s, gmlp_b_s, pool_w,
           pool_scale, even_w_out, odd_w_in, lambda_q1, lambda_k1, lambda_q2, lambda_k2, subln_gain, odd_w_out,
           ffn_w_up, ffn_conv_w, ffn_conv_b, ffn_w_down):
    row = lambda a: a.reshape(1, -1)
    bf = lambda a: a.astype(BF16)
    depth = norm_ffn.shape[0]

    later = [ffn_w_up.reshape(depth * D_MODEL, 2 * D_FF), ffn_w_down.reshape(depth * D_FF, D_MODEL),
             odd_w_in[0], odd_w_out[0], even_w_out[0]]
    y, (w_up, w_down, w_in, w_out_odd, w_out_even) = _even_mixer(
        x, row(norm_mix[0]), bf(even_w_in[0]), row(gmlp_v_gain[0]), gmlp_w_s[0], gmlp_b_s[0].T,
        bf(pool_w[0]), row(pool_scale[0]), later)
    ffn_params = (norm_ffn.reshape(depth, 1, D_MODEL), w_up.reshape(depth, D_MODEL, 2 * D_FF),
                  ffn_conv_w.reshape(depth, 3, D_FF), ffn_conv_b.reshape(depth, 1, D_FF),
                  w_down.reshape(depth, D_FF, D_MODEL), row(final_norm))
    h = _mix_ffn(x, y, y, (0, 2), (1, 2), w_out_even, *ffn_params, layer=0, final_norm=False)

    inv = ROPE_THETA ** (-jnp.arange(0, ROPE_DIM, 2, dtype=F32) / ROPE_DIM)
    qkv = _qkv_proj(h, positions, row(norm_mix[1]), inv.reshape(-1, 1), w_in[:, :C_WIDTH].T,
                    w_in[:, 2 * C_WIDTH:3 * C_WIDTH].T, w_in)
    lam_rows = jnp.stack([lambda_q1[0], lambda_k1[0], lambda_q2[0], lambda_k2[0]])
    yc = _diff_attention(qkv[0], qkv[1], qkv[2], lam_rows, subln_gain[0].reshape(-1, 1))
    yd = _dilated_attention(qkv[3:])
    return _mix_ffn(h, yc, yd, (0, 1), (0, 1), w_out_odd, *ffn_params, layer=1, final_norm=True)
```

```python
import functools
import math

import jax
import jax.numpy as jnp
from jax import lax
from jax.experimental import pallas as pl
from jax.experimental.pallas import tpu as pltpu

F32 = jnp.float32
BF16 = jnp.bfloat16

D_MODEL = 1024
A_WIDTH = 512
A_GROUPS = 4
CHUNK = 128
B_WIDTH = 512
POOL_WINDOWS = (2, 4, 8, 16)
HEAD_DIM = 64
C_HEADS = 6
C_WIDTH = 768
D_PATTERNS = ((128, 1), (512, 4), (2048, 16))
D_GROUP_WIDTH = 256
ROPE_THETA = 500000.0
ROPE_DIM = 16
D_FF = 2816
EPS = 1e-6
LAMBDA_INIT = 0.8 - 0.6 * math.exp(-0.3 * 1)

LANES = 128
HALO = 16
TOKEN_TILE = 512
FF_CHUNK = 256
ATT_Q = 512
ATT_K = 256
MXU_COLS = 256
ATT_STREAMS = 3
V_ROWS = 144
DIL_BATCH = 16
DIL_BLOCK = 128
VMEM_LIMIT = 56 * 1024 * 1024
NEG_BIG = -1e30


def _rms(x, g):
    return x * lax.rsqrt(jnp.mean(x * x, axis=-1, keepdims=True) + EPS) * g


def _gelu(x):
    c = math.sqrt(2.0 / math.pi)
    return 0.5 * x * (1.0 + jnp.tanh(c * (x + 0.044715 * (x * x * x))))


def _mm(a, b):
    return jnp.dot(a, b, preferred_element_type=F32)


def _mm_nt(a, b):
    return lax.dot_general(a, b, (((1,), (1,)), ((), ())), preferred_element_type=F32)


def _const_spec(shape):
    nd = len(shape)
    return pl.BlockSpec(shape, lambda *_: (0,) * nd, pipeline_mode=pl.Buffered(1))


def _layer_spec(shape, layer):
    nd = len(shape)
    return pl.BlockSpec((None,) + tuple(shape[1:]), lambda *_: (layer,) + (0,) * (nd - 1),
                        pipeline_mode=pl.Buffered(1))


def _params(semantics):
    return pltpu.CompilerParams(dimension_semantics=semantics, vmem_limit_bytes=VMEM_LIMIT)


def _even_kernel(h_ref, halo_ref, g_ref, win_ref, vg_ref, ws_ref, bs_ref, pw_ref, ps_ref, *refs):
    n_cast = (len(refs) - 3) // 2
    cast_in, y_ref, cast_out = refs[:n_cast], refs[n_cast], refs[n_cast + 1:2 * n_cast + 1]
    xs_ref, pb_ref = refs[2 * n_cast + 1:]
    for src, dst in zip(cast_in, cast_out):
        dst[...] = src[...].astype(BF16)
    tm = h_ref.shape[1]
    i = pl.program_id(1)
    g = g_ref[...]
    xh = jnp.where(i == 0, 0.0, halo_ref[0])
    xs_ref[0:HALO, :] = _rms(xh, g).astype(BF16)
    xs_ref[HALO:, :] = _rms(h_ref[0], g).astype(BF16)

    z = _gelu(_mm(xs_ref[HALO:, :], win_ref[:, :2 * A_WIDTH]))
    u = z[:, :A_WIDTH]
    v = _rms(z[:, A_WIDTH:], vg_ref[...]).astype(BF16)
    n_chunks = tm // CHUNK
    row = lax.broadcasted_iota(jnp.int32, (CHUNK, CHUNK), 0)
    col = lax.broadcasted_iota(jnp.int32, (CHUNK, CHUNK), 1)
    for grp in range(A_GROUPS):
        lanes = slice(grp * LANES, (grp + 1) * LANES)
        w = jnp.where(row >= col, ws_ref[grp], 0.0).astype(BF16)
        rhs = jnp.concatenate([v[c * CHUNK:(c + 1) * CHUNK, lanes] for c in range(n_chunks)], axis=1)
        mixed = _mm(w, rhs) + bs_ref[:, grp:grp + 1]
        for c in range(n_chunks):
            rows = slice(c * CHUNK, (c + 1) * CHUNK)
            y_ref[0, rows, lanes] = (u[rows, lanes] * mixed[:, c * CHUNK:(c + 1) * CHUNK]).astype(BF16)

    pb_ref[...] = _mm(xs_ref[...], win_ref[:, 2 * A_WIDTH:])
    pos = i * tm + lax.broadcasted_iota(jnp.int32, (tm, 1), 0)
    for grp, window in enumerate(POOL_WINDOWS):
        lanes = slice(grp * LANES, (grp + 1) * LANES)
        acc = pb_ref[:, lanes]
        span = 1
        while span < window:
            acc = acc + pltpu.roll(acc, span, 0)
            span *= 2
        inv_count = 1.0 / jnp.minimum(pos + 1, window).astype(F32)
        pooled = (acc[HALO:] * inv_count - pb_ref[HALO:, lanes]).astype(BF16)
        yb = _mm(pooled, pw_ref[grp]) * ps_ref[:, lanes]
        y_ref[0, :, A_WIDTH + grp * LANES:A_WIDTH + (grp + 1) * LANES] = yb.astype(BF16)


def _even_mixer(h, gain, w_in, v_gain, w_s, b_s, pool_w, pool_scale, later_weights):
    bn, s, _ = h.shape
    tm = TOKEN_TILE
    n_steps = bn * (s // tm)
    main = pl.BlockSpec((1, tm, D_MODEL), lambda b, i: (b, i, 0))
    halo = pl.BlockSpec((1, HALO, D_MODEL), lambda b, i: (b, jnp.maximum(i * (tm // HALO) - 1, 0), 0))
    slabs = [pl.BlockSpec((w.shape[0] // n_steps, w.shape[1]), lambda b, i: (b * (s // tm) + i, 0))
             for w in later_weights]
    outs = pl.pallas_call(
        _even_kernel,
        grid=(bn, s // tm),
        in_specs=[main, halo, _const_spec((1, D_MODEL)), _const_spec(w_in.shape), _const_spec((1, A_WIDTH)),
                  _const_spec(w_s.shape), _const_spec(b_s.shape), _const_spec(pool_w.shape),
                  _const_spec((1, B_WIDTH))] + slabs,
        out_specs=[main] + slabs,
        out_shape=[jax.ShapeDtypeStruct(h.shape, BF16)] + [jax.ShapeDtypeStruct(w.shape, BF16) for w in later_weights],
        scratch_shapes=[pltpu.VMEM((tm + HALO, D_MODEL), BF16), pltpu.VMEM((tm + HALO, B_WIDTH), F32)],
        compiler_params=_params(("parallel", "parallel")),
        name="even_mixer",
    )(h, h, gain, w_in, v_gain, w_s, b_s, pool_w, pool_scale, *later_weights)
    return outs[0], outs[1:]


def _ffn_kernel(h_ref, ya_ref, yb_ref, wo_ref, g_ref, wup_ref, cw_ref, cb_ref, wdown_ref, fg_ref, o_ref,
                xs_ref, a_ref, hg_ref, *, final_norm):
    tm = h_ref.shape[1]
    wa = ya_ref.shape[2]
    i = pl.program_id(1)

    @pl.when(i == 0)
    def _():
        xs_ref[0:HALO, :] = jnp.zeros((HALO, D_MODEL), BF16)

    @pl.when(i > 0)
    def _():
        xs_ref[0:HALO, :] = xs_ref[tm:tm + HALO, :]

    hm = h_ref[0] + _mm(ya_ref[0], wo_ref[0:wa, :]) + _mm(yb_ref[0], wo_ref[wa:, :])
    xs_ref[HALO:, :] = _rms(hm, g_ref[...]).astype(BF16)
    for c in range(D_FF // FF_CHUNK):
        cols = slice(c * FF_CHUNK, (c + 1) * FF_CHUNK)
        a_ref[...] = _mm(xs_ref[...], wup_ref[:, cols])
        gate = _mm(xs_ref[HALO:, :], wup_ref[:, D_FF + c * FF_CHUNK:D_FF + (c + 1) * FF_CHUNK])
        conv = (a_ref[pl.ds(HALO - 2, tm), :] * cw_ref[0:1, cols]
                + a_ref[pl.ds(HALO - 1, tm), :] * cw_ref[1:2, cols]
                + a_ref[pl.ds(HALO, tm), :] * cw_ref[2:3, cols]
                + cb_ref[:, cols])
        hg_ref[:, cols] = (_gelu(conv) * gate).astype(BF16)
    y = hm + _mm(hg_ref[...], wdown_ref[...])
    if final_norm:
        y = _rms(y, fg_ref[...])
    o_ref[0] = y


def _mix_ffn(h, ya, yb, col_a, col_b, w_out, gains, w_up, conv_w, conv_b, w_down, final_gain, layer, final_norm):
    bn, s, _ = h.shape
    tm = TOKEN_TILE
    wa, wb = ya.shape[2] // (col_a[1]), yb.shape[2] // (col_b[1])
    main = pl.BlockSpec((1, tm, D_MODEL), lambda b, i: (b, i, 0))
    return pl.pallas_call(
        functools.partial(_ffn_kernel, final_norm=final_norm),
        grid=(bn, s // tm),
        in_specs=[main,
                  pl.BlockSpec((1, tm, wa), lambda b, i: (b, i, col_a[0])),
                  pl.BlockSpec((1, tm, wb), lambda b, i: (b, i, col_b[0])),
                  _const_spec(w_out.shape), _layer_spec(gains.shape, layer), _layer_spec(w_up.shape, layer),
                  _layer_spec(conv_w.shape, layer), _layer_spec(conv_b.shape, layer),
                  _layer_spec(w_down.shape, layer), _const_spec((1, D_MODEL))],
        out_specs=main,
        out_shape=jax.ShapeDtypeStruct(h.shape, F32),
        scratch_shapes=[pltpu.VMEM((tm + HALO, D_MODEL), BF16), pltpu.VMEM((tm + HALO, FF_CHUNK), F32),
                        pltpu.VMEM((tm, D_FF), BF16)],
        compiler_params=_params(("parallel", "arbitrary")),
        name="mix_ffn",
    )(h, ya, yb, w_out, gains, w_up, conv_w, conv_b, w_down, final_gain)


def _rope(x, tab):
    c, s_up, s_lo = tab
    return x * c + pltpu.roll(x, ROPE_DIM // 2, 1) * s_up + pltpu.roll(x, LANES - ROPE_DIM // 2, 1) * s_lo


def _qkv_kernel(h_ref, pos_ref, g_ref, inv_ref, wqt_ref, wvt_ref, wn_ref,
                cqt_ref, ck_ref, cvt_ref, dq0_ref, dk0_ref, dv0_ref, dq1_ref, dk1_ref, dv1_ref,
                dq2_ref, dk2_ref, dv2_ref, xs_ref, tab_ref, perm_ref):
    tm = h_ref.shape[1]
    half = ROPE_DIM // 2
    xs_ref[...] = _rms(h_ref[0], g_ref[...]).astype(BF16)

    ang = inv_ref[...] * pos_ref[0].astype(F32)
    cos_t, sin_t = jnp.cos(ang), jnp.sin(ang)
    rest = HEAD_DIM - ROPE_DIM
    ones, z_half, z_rest = jnp.ones((rest, tm), F32), jnp.zeros((half, tm), F32), jnp.zeros((rest, tm), F32)
    tab_ref[0] = jnp.concatenate([cos_t, cos_t, ones] * 2, axis=0).T
    tab_ref[1] = jnp.concatenate([z_half, sin_t, z_rest] * 2, axis=0).T
    tab_ref[2] = jnp.concatenate([-sin_t, z_half, z_rest] * 2, axis=0).T

    tabs = (tab_ref[0], tab_ref[1], tab_ref[2])

    def project(col0, width, rope, scale):
        outs = []
        for c in range(width // MXU_COLS):
            r = _mm(xs_ref[...], wn_ref[:, col0 + c * MXU_COLS:col0 + (c + 1) * MXU_COLS])
            for x in (r[:, :LANES], r[:, LANES:]):
                if rope:
                    x = _rope(x, tabs)
                if scale is not None:
                    x = x * scale
                outs.append(x)
        return outs

    dscale = HEAD_DIM ** -0.5 * math.log2(math.e)
    gw = D_GROUP_WIDTH
    slot = 0
    group_refs = ((dq0_ref, dk0_ref, dv0_ref), (dq1_ref, dk1_ref, dv1_ref), (dq2_ref, dk2_ref, dv2_ref))
    for grp in reversed(range(len(D_PATTERNS))):
        dil = D_PATTERNS[grp][1]
        n = tm // dil
        for part in (2, 0, 1):
            ref, rope, scale = group_refs[grp][part], part < 2, dscale if part == 0 else None
            col0 = 3 * C_WIDTH + part * len(D_PATTERNS) * gw + grp * gw
            for c, res in enumerate(project(col0, gw, rope, scale)):
                lanes = slice(c * LANES, (c + 1) * LANES)
                if dil == 1:
                    ref[0, 0, :, lanes] = res.astype(BF16)
                else:
                    perm_ref[slot] = res
                    for r in range(dil):
                        ref[0, r, :, lanes] = perm_ref[slot, pl.ds(r, n, stride=dil), :].astype(BF16)
                    slot += 1

    for c, r in enumerate(project(C_WIDTH, C_WIDTH, True, None)):
        ck_ref[0, :, c * LANES:(c + 1) * LANES] = r.astype(BF16)

    qt = _mm_nt(wqt_ref[...], xs_ref[...])
    pieces = []
    for hd in range(C_WIDTH // HEAD_DIM):
        x1 = qt[hd * HEAD_DIM:hd * HEAD_DIM + half]
        x2 = qt[hd * HEAD_DIM + half:hd * HEAD_DIM + ROPE_DIM]
        pieces += [x1 * cos_t - x2 * sin_t, x2 * cos_t + x1 * sin_t, qt[hd * HEAD_DIM + ROPE_DIM:(hd + 1) * HEAD_DIM]]
    qt = jnp.concatenate(pieces, axis=0) * (HEAD_DIM ** -0.5 * math.log2(math.e))
    for hp in range(C_HEADS):
        cqt_ref[0, hp, 0] = qt[hp * LANES:(hp + 1) * LANES].astype(BF16)

    vt = _mm_nt(wvt_ref[...], xs_ref[...])
    for hp in range(C_HEADS):
        for j in range(tm // ATT_K):
            cvt_ref[0, hp, j] = vt[hp * LANES:(hp + 1) * LANES, j * ATT_K:(j + 1) * ATT_K].astype(BF16)


def _qkv_proj(h, positions, gain, inv_col, wq_t, wv_t, w_nat):
    bn, s, _ = h.shape
    tm = TOKEN_TILE
    main = pl.BlockSpec((1, tm, D_MODEL), lambda b, i: (b, i, 0))
    pos_spec = pl.BlockSpec((1, 1, tm), lambda b, i: (b, 0, i))
    out_shapes = [jax.ShapeDtypeStruct((bn, C_HEADS, s // ATT_Q, LANES, ATT_Q), BF16),
                  jax.ShapeDtypeStruct((bn, s, C_WIDTH), BF16),
                  jax.ShapeDtypeStruct((bn, C_HEADS, s // ATT_K, LANES, ATT_K), BF16)]
    out_specs = [pl.BlockSpec((1, C_HEADS, tm // ATT_Q, LANES, ATT_Q), lambda b, i: (b, 0, i, 0, 0)),
                 pl.BlockSpec((1, tm, C_WIDTH), lambda b, i: (b, i, 0)),
                 pl.BlockSpec((1, C_HEADS, tm // ATT_K, LANES, ATT_K), lambda b, i: (b, 0, i, 0, 0))]
    n_perm = 0
    for _, dil in D_PATTERNS:
        out_shapes += [jax.ShapeDtypeStruct((bn, dil, s // dil, D_GROUP_WIDTH), BF16)] * 3
        out_specs += [pl.BlockSpec((1, dil, tm // dil, D_GROUP_WIDTH), lambda b, i: (b, 0, i, 0))] * 3
        n_perm += 0 if dil == 1 else 3 * D_GROUP_WIDTH // LANES
    return pl.pallas_call(
        _qkv_kernel,
        grid=(bn, s // tm),
        in_specs=[main, pos_spec, _const_spec((1, D_MODEL)), _const_spec(inv_col.shape), _const_spec(wq_t.shape),
                  _const_spec(wv_t.shape), _const_spec(w_nat.shape)],
        out_specs=out_specs,
        out_shape=out_shapes,
        scratch_shapes=[pltpu.VMEM((tm, D_MODEL), BF16), pltpu.VMEM((3, tm, LANES), F32),
                        pltpu.VMEM((n_perm, tm, LANES), F32)],
        compiler_params=_params(("parallel", "parallel")),
        name="qkv_proj",
    )(h, positions.reshape(bn, 1, s), gain, inv_col, wq_t, wv_t, w_nat)


def _diff_kernel(qt_ref, k_ref, vt_ref, lam_ref, sg_ref, o_ref, qc_ref, s_ref, cm_ref, e_ref, m_ref, acc_ref):
    tq, tk = ATT_Q, ATT_K
    streams = range(qt_ref.shape[1])
    n_strips = 2 * tq // LANES
    all_strips = tuple(range(n_strips))
    late = tuple(j for j in all_strips if (j * LANES) % tq >= tk)
    ones = jnp.ones((V_ROWS - LANES, tk), BF16)
    row = lax.broadcasted_iota(jnp.int32, (LANES, tq), 0)
    lam_rows = lam_ref[...]
    lam = (jnp.exp(jnp.sum(lam_rows[0:1] * lam_rows[1:2], axis=-1, keepdims=True))
           - jnp.exp(jnp.sum(lam_rows[2:3] * lam_rows[3:4], axis=-1, keepdims=True)) + LAMBDA_INIT)
    out_gain = sg_ref[...] * (1.0 - LAMBDA_INIT)

    def qk(st, kb, buf, strips=all_strips):
        start = pl.multiple_of(kb * tk, tk)
        rhs = qc_ref[st] if strips == all_strips else jnp.concatenate(
            [qc_ref[st, :, j * LANES:(j + 1) * LANES] for j in strips], axis=1)
        s = _mm(k_ref[0, pl.ds(start, tk), st * LANES:(st + 1) * LANES], rhs)
        for idx, j in enumerate(strips):
            strip = s[:, idx * LANES:(idx + 1) * LANES]
            s_ref[st, buf, j] = strip
            cm_ref[st, buf, :, j * LANES:(j + 1) * LANES] = jnp.max(strip, axis=0, keepdims=True)

    def soft(st, buf, kb, strips=all_strips, key_off=None):
        alphas = []
        for j in strips:
            cols = slice(j * LANES, (j + 1) * LANES)
            s = s_ref[st, buf, j]
            q_rel = (j * LANES) % tq
            col_max = cm_ref[st, buf, :, cols]
            if key_off is not None and q_rel < key_off + tk:
                key = key_off + lax.broadcasted_iota(jnp.int32, (tk, LANES), 0)
                qry = q_rel + lax.broadcasted_iota(jnp.int32, (tk, LANES), 1)
                s = jnp.where(key <= qry, s, NEG_BIG)
                col_max = jnp.max(s, axis=0, keepdims=True)
            m_old = m_ref[st, :, cols]
            m_new = jnp.maximum(m_old, col_max)
            alphas.append(jnp.exp2(m_old - m_new))
            m_ref[st, :, cols] = m_new
            e_ref[st, j] = jnp.exp2(s - m_new).astype(BF16)
        lhs = jnp.concatenate([vt_ref[0, st, kb], ones], axis=0)
        pv = _mm(lhs, jnp.concatenate([e_ref[st, j] for j in strips], axis=1))
        if strips == all_strips:
            acc_ref[st] = acc_ref[st] * jnp.concatenate(alphas, axis=1) + pv
        else:
            for idx, j in enumerate(strips):
                cols = slice(j * LANES, (j + 1) * LANES)
                acc_ref[st, :, cols] = acc_ref[st, :, cols] * alphas[idx] + pv[:, idx * LANES:(idx + 1) * LANES]

    def load_queries(st, qj):
        qt = qt_ref[0, st, qj]
        zero = jnp.zeros_like(qt)
        qc_ref[st, :, :tq] = jnp.where(row < HEAD_DIM, qt, zero)
        qc_ref[st, :, tq:] = jnp.where(row >= HEAD_DIM, qt, zero)

    def finish(st, qj):
        inv = 1.0 / acc_ref[st, LANES:LANES + 1, :]
        ot = acc_ref[st, 0:LANES, :tq] * inv[:, :tq] - lam * (acc_ref[st, 0:LANES, tq:] * inv[:, tq:])
        ot = ot * lax.rsqrt(jnp.mean(ot * ot, axis=0, keepdims=True) + EPS) * out_gain
        o_ref[0, pl.ds(pl.multiple_of(qj * tq, tq), tq), st * LANES:(st + 1) * LANES] = ot.T.astype(BF16)

    n_blocks = qt_ref.shape[2]
    for st in streams:
        load_queries(st, 0)
        qk(st, 0, 0)

    def query_block(qj, _):
        m_ref[...] = jnp.full(m_ref.shape, NEG_BIG, F32)
        acc_ref[...] = jnp.zeros(acc_ref.shape, F32)

        def pair(it, _):
            ns = len(streams)
            for st in streams:
                qk(st, 2 * it + 1, 1)
            for st in range(ns + 1):
                if st < ns:
                    soft(st, 0, 2 * it)
                    qk(st, 2 * it + 2, 0)
                if st >= 1:
                    soft(st - 1, 1, 2 * it + 1)
            return 0

        lax.fori_loop(0, qj, pair, 0)
        for st in streams:
            qk(st, 2 * qj + 1, 1, late)
            soft(st, 0, 2 * qj, key_off=0)
            load_queries(st, jnp.minimum(qj + 1, n_blocks - 1))
            qk(st, 0, 0)
            soft(st, 1, 2 * qj + 1, late, key_off=tk)
            finish(st, qj)
        return 0

    lax.fori_loop(0, n_blocks, query_block, 0)


def _diff_attention(cqt, ck, cvt, lam_rows, subln_gain_col):
    bn, s, _ = ck.shape
    tq, tk, ns = ATT_Q, ATT_K, ATT_STREAMS
    return pl.pallas_call(
        _diff_kernel,
        grid=(bn, C_HEADS // ns),
        in_specs=[pl.BlockSpec((1, ns, s // tq, LANES, tq), lambda b, h: (b, h, 0, 0, 0)),
                  pl.BlockSpec((1, s, ns * LANES), lambda b, h: (b, 0, h)),
                  pl.BlockSpec((1, ns, s // tk, LANES, tk), lambda b, h: (b, h, 0, 0, 0)),
                  pl.BlockSpec(lam_rows.shape, lambda b, h: (0, 0)),
                  pl.BlockSpec((LANES, 1), lambda b, h: (0, 0))],
        out_specs=pl.BlockSpec((1, s, ns * LANES), lambda b, h: (b, 0, h)),
        out_shape=jax.ShapeDtypeStruct((bn, s, C_WIDTH), BF16),
        scratch_shapes=[pltpu.VMEM((ns, LANES, 2 * tq), BF16), pltpu.VMEM((ns, 2, 2 * tq // LANES, tk, LANES), F32),
                        pltpu.VMEM((ns, 2, 1, 2 * tq), F32),
                        pltpu.VMEM((ns, 2 * tq // LANES, tk, LANES), BF16), pltpu.VMEM((ns, 1, 2 * tq), F32),
                        pltpu.VMEM((ns, V_ROWS, 2 * tq), F32)],
        compiler_params=_params(("parallel", "parallel")),
        name="diff_attention",
    )(cqt, ck, cvt, lam_rows, subln_gain_col)


def _dil_kernel(q0, k0, v0, q1, k1, v1, q2, k2, v2, o_ref, os_ref, ls_ref):
    nk = DIL_BLOCK
    s_len = o_ref.shape[1]
    lane = lax.broadcasted_iota(jnp.int32, (nk, LANES), 1)
    first_head = lane < HEAD_DIM
    row = lax.broadcasted_iota(jnp.int32, (2 * nk, 2 * nk), 0) % nk
    col = lax.broadcasted_iota(jnp.int32, (2 * nk, 2 * nk), 1)
    band = ((col < nk) & (col >= row)) | ((col >= nk) & (col - nk <= row))
    ones = jnp.ones((2 * nk, LANES), BF16)

    for grp, ((_, dil), q_ref, k_ref, v_ref) in enumerate(zip(D_PATTERNS, (q0, q1, q2), (k0, k1, k2), (v0, v1, v2))):
        nb = s_len // dil // nk
        per_res = min(nb, DIL_BATCH)
        n_res = DIL_BATCH // per_res
        chunks = nb // per_res

        def batch(t, _, dil=dil, per_res=per_res, n_res=n_res, chunks=chunks, grp=grp,
                  q_ref=q_ref, k_ref=k_ref, v_ref=v_ref):
            n0 = (t % chunks) * per_res
            base = pl.multiple_of(n0 * nk, nk)
            prev = pl.multiple_of(jnp.maximum(n0 - 1, 0) * nk, nk)
            for rr in range(n_res):
                r = (t // chunks) * n_res + rr
                q_all = q_ref[0, r, pl.ds(base, per_res * nk), :]
                k_all = jnp.concatenate([k_ref[0, r, pl.ds(prev, nk), :], k_ref[0, r, pl.ds(base, per_res * nk), :]], axis=0)
                v_all = jnp.concatenate([v_ref[0, r, pl.ds(prev, nk), :], v_ref[0, r, pl.ds(base, per_res * nk), :]], axis=0)
                for i in range(per_res):
                    q = q_all[i * nk:(i + 1) * nk]
                    k = k_all[i * nk:(i + 2) * nk]
                    v = jnp.concatenate([v_all[i * nk:(i + 2) * nk], ones], axis=1)
                    mask = band
                    if i == 0:
                        mask = band & (col >= jnp.where(n0 == 0, nk, 0))
                    zero = jnp.zeros_like(q)
                    q2 = jnp.concatenate([jnp.where(first_head, q, zero), jnp.where(first_head, zero, q)], axis=0)
                    s = jnp.where(mask, _mm_nt(q2, k), NEG_BIG)
                    m = jnp.max(s, axis=-1, keepdims=True)
                    pv = _mm(jnp.exp2(s - m).astype(BF16), v)
                    num = jnp.where(first_head, pv[:nk, :LANES], pv[nk:, :LANES])
                    den = jnp.where(first_head, pv[:nk, LANES:], pv[nk:, LANES:])
                    top = jnp.where(first_head, jnp.broadcast_to(m[:nk], (nk, LANES)), jnp.broadcast_to(m[nk:], (nk, LANES)))
                    rows = pl.ds((n0 + i) * (nk * dil) + r, nk, stride=dil)
                    os_ref[grp, rows, :] = num / den
                    ls_ref[grp, rows, :] = top + jnp.log2(den)
            return 0

        lax.fori_loop(0, (dil // n_res) * chunks, batch, 0)

    tile = 512
    for t in range(s_len // tile):
        rows = slice(t * tile, (t + 1) * tile)
        lse = [ls_ref[grp, rows, :] for grp in range(3)]
        top = jnp.maximum(jnp.maximum(lse[0], lse[1]), lse[2])
        wts = [jnp.exp2(x - top) for x in lse]
        num = wts[0] * os_ref[0, rows, :] + wts[1] * os_ref[1, rows, :] + wts[2] * os_ref[2, rows, :]
        o_ref[0, rows, :] = (num / (wts[0] + wts[1] + wts[2])).astype(BF16)


def _dilated_attention(dqkv):
    bn = dqkv[0].shape[0]
    s = dqkv[0].shape[2]
    specs = []
    for (_, dil) in D_PATTERNS:
        specs += [pl.BlockSpec((1, dil, s // dil, LANES), lambda b, hp: (b, 0, 0, hp))] * 3
    return pl.pallas_call(
        _dil_kernel,
        grid=(bn, D_GROUP_WIDTH // LANES),
        in_specs=specs,
        out_specs=pl.BlockSpec((1, s, LANES), lambda b, hp: (b, 0, hp)),
        out_shape=jax.ShapeDtypeStruct((bn, s, D_GROUP_WIDTH), BF16),
        scratch_shapes=[pltpu.VMEM((3, s, LANES), F32), pltpu.VMEM((3, s, LANES), F32)],
        compiler_params=_params(("parallel", "parallel")),
        name="dilated_attention",
    )(*dqkv)


def kernel(x, positions, norm_mix, norm_ffn, final_norm, even_w_in, gmlp_v_gain, gmlp_w_s, gmlp_b_s, pool_w,
           pool_scale, even_w_out, odd_w_in, lambda_q1, lambda_k1, lambda_q2, lambda_k2, subln_gain, odd_w_out,
           ffn_w_up, ffn_conv_w, ffn_conv_b, ffn_w_down):
    row = lambda a: a.reshape(1, -1)
    bf = lambda a: a.astype(BF16)
    depth = norm_ffn.shape[0]

    later = [ffn_w_up.reshape(depth * D_MODEL, 2 * D_FF), ffn_w_down.reshape(depth * D_FF, D_MODEL),
             odd_w_in[0], odd_w_out[0], even_w_out[0]]
    y, (w_up, w_down, w_in, w_out_odd, w_out_even) = _even_mixer(
        x, row(norm_mix[0]), bf(even_w_in[0]), row(gmlp_v_gain[0]), gmlp_w_s[0], gmlp_b_s[0].T,
        bf(pool_w[0]), row(pool_scale[0]), later)
    ffn_params = (norm_ffn.reshape(depth, 1, D_MODEL), w_up.reshape(depth, D_MODEL, 2 * D_FF),
                  ffn_conv_w.reshape(depth, 3, D_FF), ffn_conv_b.reshape(depth, 1, D_FF),
                  w_down.reshape(depth, D_FF, D_MODEL), row(final_norm))
    h = _mix_ffn(x, y, y, (0, 2), (1, 2), w_out_even, *ffn_params, layer=0, final_norm=False)

    inv = ROPE_THETA ** (-jnp.arange(0, ROPE_DIM, 2, dtype=F32) / ROPE_DIM)
    qkv = _qkv_proj(h, positions, row(norm_mix[1]), inv.reshape(-1, 1), w_in[:, :C_WIDTH].T,
                    w_in[:, 2 * C_WIDTH:3 * C_WIDTH].T, w_in)
    lam_rows = jnp.stack([lambda_q1[0], lambda_k1[0], lambda_q2[0], lambda_k2[0]])
    yc = _diff_attention(qkv[0], qkv[1], qkv[2], lam_rows, subln_gain[0].reshape(-1, 1))
    yd = _dilated_attention(qkv[3:])
    return _mix_ffn(h, yc, yd, (0, 1), (0, 1), w_out_odd, *ffn_params, layer=1, final_norm=True)
```

```python
import functools
import math

import jax
import jax.numpy as jnp
from jax import lax
from jax.experimental import pallas as pl
from jax.experimental.pallas import tpu as pltpu

F32 = jnp.float32
BF16 = jnp.bfloat16

D_MODEL = 1024
A_WIDTH = 512
A_GROUPS = 4
CHUNK = 128
B_WIDTH = 512
POOL_WINDOWS = (2, 4, 8, 16)
HEAD_DIM = 64
C_HEADS = 6
C_WIDTH = 768
D_PATTERNS = ((128, 1), (512, 4), (2048, 16))
D_GROUP_WIDTH = 256
ROPE_THETA = 500000.0
ROPE_DIM = 16
D_FF = 2816
EPS = 1e-6
LAMBDA_INIT = 0.8 - 0.6 * math.exp(-0.3 * 1)

LANES = 128
HALO = 16
TOKEN_TILE = 512
FF_CHUNK = 256
ATT_Q = 512
ATT_K = 256
MXU_COLS = 256
ATT_STREAMS = 3
STRIP_PAD = 8
V_ROWS = 144
DIL_BATCH = 16
DIL_BLOCK = 128
VMEM_LIMIT = 56 * 1024 * 1024
NEG_BIG = -1e30


def _rms(x, g):
    return x * lax.rsqrt(jnp.mean(x * x, axis=-1, keepdims=True) + EPS) * g


def _gelu(x):
    c = math.sqrt(2.0 / math.pi)
    return 0.5 * x * (1.0 + jnp.tanh(c * (x + 0.044715 * (x * x * x))))


def _mm(a, b):
    return jnp.dot(a, b, preferred_element_type=F32)


def _mm_nt(a, b):
    return lax.dot_general(a, b, (((1,), (1,)), ((), ())), preferred_element_type=F32)


def _const_spec(shape):
    nd = len(shape)
    return pl.BlockSpec(shape, lambda *_: (0,) * nd, pipeline_mode=pl.Buffered(1))


def _layer_spec(shape, layer):
    nd = len(shape)
    return pl.BlockSpec((None,) + tuple(shape[1:]), lambda *_: (layer,) + (0,) * (nd - 1),
                        pipeline_mode=pl.Buffered(1))


def _params(semantics):
    return pltpu.CompilerParams(dimension_semantics=semantics, vmem_limit_bytes=VMEM_LIMIT)


def _even_kernel(h_ref, halo_ref, g_ref, win_ref, vg_ref, ws_ref, bs_ref, pw_ref, ps_ref, *refs):
    n_cast = (len(refs) - 3) // 2
    cast_in, y_ref, cast_out = refs[:n_cast], refs[n_cast], refs[n_cast + 1:2 * n_cast + 1]
    xs_ref, pb_ref = refs[2 * n_cast + 1:]
    for src, dst in zip(cast_in, cast_out):
        dst[...] = src[...].astype(BF16)
    tm = h_ref.shape[1]
    i = pl.program_id(1)
    g = g_ref[...]
    xh = jnp.where(i == 0, 0.0, halo_ref[0])
    xs_ref[0:HALO, :] = _rms(xh, g).astype(BF16)
    xs_ref[HALO:, :] = _rms(h_ref[0], g).astype(BF16)

    z = _gelu(_mm(xs_ref[HALO:, :], win_ref[:, :2 * A_WIDTH]))
    u = z[:, :A_WIDTH]
    v = _rms(z[:, A_WIDTH:], vg_ref[...]).astype(BF16)
    n_chunks = tm // CHUNK
    row = lax.broadcasted_iota(jnp.int32, (CHUNK, CHUNK), 0)
    col = lax.broadcasted_iota(jnp.int32, (CHUNK, CHUNK), 1)
    for grp in range(A_GROUPS):
        lanes = slice(grp * LANES, (grp + 1) * LANES)
        w = jnp.where(row >= col, ws_ref[grp], 0.0).astype(BF16)
        rhs = jnp.concatenate([v[c * CHUNK:(c + 1) * CHUNK, lanes] for c in range(n_chunks)], axis=1)
        mixed = _mm(w, rhs) + bs_ref[:, grp:grp + 1]
        for c in range(n_chunks):
            rows = slice(c * CHUNK, (c + 1) * CHUNK)
            y_ref[0, rows, lanes] = (u[rows, lanes] * mixed[:, c * CHUNK:(c + 1) * CHUNK]).astype(BF16)

    pb_ref[...] = _mm(xs_ref[...], win_ref[:, 2 * A_WIDTH:])
    pos = i * tm + lax.broadcasted_iota(jnp.int32, (tm, 1), 0)
    for grp, window in enumerate(POOL_WINDOWS):
        lanes = slice(grp * LANES, (grp + 1) * LANES)
        acc = pb_ref[:, lanes]
        span = 1
        while span < window:
            acc = acc + pltpu.roll(acc, span, 0)
            span *= 2
        inv_count = 1.0 / jnp.minimum(pos + 1, window).astype(F32)
        pooled = (acc[HALO:] * inv_count - pb_ref[HALO:, lanes]).astype(BF16)
        yb = _mm(pooled, pw_ref[grp]) * ps_ref[:, lanes]
        y_ref[0, :, A_WIDTH + grp * LANES:A_WIDTH + (grp + 1) * LANES] = yb.astype(BF16)


def _even_mixer(h, gain, w_in, v_gain, w_s, b_s, pool_w, pool_scale, later_weights):
    bn, s, _ = h.shape
    tm = TOKEN_TILE
    n_steps = bn * (s // tm)
    main = pl.BlockSpec((1, tm, D_MODEL), lambda b, i: (b, i, 0))
    halo = pl.BlockSpec((1, HALO, D_MODEL), lambda b, i: (b, jnp.maximum(i * (tm // HALO) - 1, 0), 0))
    slabs = [pl.BlockSpec((w.shape[0] // n_steps, w.shape[1]), lambda b, i: (b * (s // tm) + i, 0))
             for w in later_weights]
    outs = pl.pallas_call(
        _even_kernel,
        grid=(bn, s // tm),
        in_specs=[main, halo, _const_spec((1, D_MODEL)), _const_spec(w_in.shape), _const_spec((1, A_WIDTH)),
                  _const_spec(w_s.shape), _const_spec(b_s.shape), _const_spec(pool_w.shape),
                  _const_spec((1, B_WIDTH))] + slabs,
        out_specs=[main] + slabs,
        out_shape=[jax.ShapeDtypeStruct(h.shape, BF16)] + [jax.ShapeDtypeStruct(w.shape, BF16) for w in later_weights],
        scratch_shapes=[pltpu.VMEM((tm + HALO, D_MODEL), BF16), pltpu.VMEM((tm + HALO, B_WIDTH), F32)],
        compiler_params=_params(("parallel", "parallel")),
        name="even_mixer",
    )(h, h, gain, w_in, v_gain, w_s, b_s, pool_w, pool_scale, *later_weights)
    return outs[0], outs[1:]


def _ffn_kernel(h_ref, ya_ref, yb_ref, wo_ref, g_ref, wup_ref, cw_ref, cb_ref, wdown_ref, fg_ref, o_ref,
                xs_ref, a_ref, hg_ref, *, final_norm):
    tm = h_ref.shape[1]
    wa = ya_ref.shape[2]
    i = pl.program_id(1)

    @pl.when(i == 0)
    def _():
        xs_ref[0:HALO, :] = jnp.zeros((HALO, D_MODEL), BF16)

    @pl.when(i > 0)
    def _():
        xs_ref[0:HALO, :] = xs_ref[tm:tm + HALO, :]

    hm = h_ref[0] + _mm(ya_ref[0], wo_ref[0:wa, :]) + _mm(yb_ref[0], wo_ref[wa:, :])
    xs_ref[HALO:, :] = _rms(hm, g_ref[...]).astype(BF16)
    for c in range(D_FF // FF_CHUNK):
        cols = slice(c * FF_CHUNK, (c + 1) * FF_CHUNK)
        a_ref[...] = _mm(xs_ref[...], wup_ref[:, cols])
        gate = _mm(xs_ref[HALO:, :], wup_ref[:, D_FF + c * FF_CHUNK:D_FF + (c + 1) * FF_CHUNK])
        conv = (a_ref[pl.ds(HALO - 2, tm), :] * cw_ref[0:1, cols]
                + a_ref[pl.ds(HALO - 1, tm), :] * cw_ref[1:2, cols]
                + a_ref[pl.ds(HALO, tm), :] * cw_ref[2:3, cols]
                + cb_ref[:, cols])
        hg_ref[:, cols] = (_gelu(conv) * gate).astype(BF16)
    y = hm + _mm(hg_ref[...], wdown_ref[...])
    if final_norm:
        y = _rms(y, fg_ref[...])
    o_ref[0] = y


def _mix_ffn(h, ya, yb, col_a, col_b, w_out, gains, w_up, conv_w, conv_b, w_down, final_gain, layer, final_norm):
    bn, s, _ = h.shape
    tm = TOKEN_TILE
    wa, wb = ya.shape[2] // (col_a[1]), yb.shape[2] // (col_b[1])
    main = pl.BlockSpec((1, tm, D_MODEL), lambda b, i: (b, i, 0))
    return pl.pallas_call(
        functools.partial(_ffn_kernel, final_norm=final_norm),
        grid=(bn, s // tm),
        in_specs=[main,
                  pl.BlockSpec((1, tm, wa), lambda b, i: (b, i, col_a[0])),
                  pl.BlockSpec((1, tm, wb), lambda b, i: (b, i, col_b[0])),
                  _const_spec(w_out.shape), _layer_spec(gains.shape, layer), _layer_spec(w_up.shape, layer),
                  _layer_spec(conv_w.shape, layer), _layer_spec(conv_b.shape, layer),
                  _layer_spec(w_down.shape, layer), _const_spec((1, D_MODEL))],
        out_specs=main,
        out_shape=jax.ShapeDtypeStruct(h.shape, F32),
        scratch_shapes=[pltpu.VMEM((tm + HALO, D_MODEL), BF16), pltpu.VMEM((tm + HALO, FF_CHUNK), F32),
                        pltpu.VMEM((tm, D_FF), BF16)],
        compiler_params=_params(("parallel", "arbitrary")),
        name="mix_ffn",
    )(h, ya, yb, w_out, gains, w_up, conv_w, conv_b, w_down, final_gain)


def _rope(x, tab):
    c, s_up, s_lo = tab
    return x * c + pltpu.roll(x, ROPE_DIM // 2, 1) * s_up + pltpu.roll(x, LANES - ROPE_DIM // 2, 1) * s_lo


def _qkv_kernel(h_ref, pos_ref, g_ref, inv_ref, wqt_ref, wvt_ref, wn_ref,
                cqt_ref, ck_ref, cvt_ref, dq0_ref, dk0_ref, dv0_ref, dq1_ref, dk1_ref, dv1_ref,
                dq2_ref, dk2_ref, dv2_ref, xs_ref, tab_ref, perm_ref):
    tm = h_ref.shape[1]
    half = ROPE_DIM // 2
    xs_ref[...] = _rms(h_ref[0], g_ref[...]).astype(BF16)

    ang = inv_ref[...] * pos_ref[0].astype(F32)
    cos_t, sin_t = jnp.cos(ang), jnp.sin(ang)
    rest = HEAD_DIM - ROPE_DIM
    ones, z_half, z_rest = jnp.ones((rest, tm), F32), jnp.zeros((half, tm), F32), jnp.zeros((rest, tm), F32)
    tab_ref[0] = jnp.concatenate([cos_t, cos_t, ones] * 2, axis=0).T
    tab_ref[1] = jnp.concatenate([z_half, sin_t, z_rest] * 2, axis=0).T
    tab_ref[2] = jnp.concatenate([-sin_t, z_half, z_rest] * 2, axis=0).T

    tabs = (tab_ref[0], tab_ref[1], tab_ref[2])

    def project(col0, width, rope, scale):
        outs = []
        for c in range(width // MXU_COLS):
            r = _mm(xs_ref[...], wn_ref[:, col0 + c * MXU_COLS:col0 + (c + 1) * MXU_COLS])
            for x in (r[:, :LANES], r[:, LANES:]):
                if rope:
                    x = _rope(x, tabs)
                if scale is not None:
                    x = x * scale
                outs.append(x)
        return outs

    dscale = HEAD_DIM ** -0.5 * math.log2(math.e)
    gw = D_GROUP_WIDTH
    slot = 0
    group_refs = ((dq0_ref, dk0_ref, dv0_ref), (dq1_ref, dk1_ref, dv1_ref), (dq2_ref, dk2_ref, dv2_ref))
    for grp in reversed(range(len(D_PATTERNS))):
        dil = D_PATTERNS[grp][1]
        n = tm // dil
        for part in (2, 0, 1):
            ref, rope, scale = group_refs[grp][part], part < 2, dscale if part == 0 else None
            col0 = 3 * C_WIDTH + part * len(D_PATTERNS) * gw + grp * gw
            for c, res in enumerate(project(col0, gw, rope, scale)):
                lanes = slice(c * LANES, (c + 1) * LANES)
                if dil == 1:
                    ref[0, 0, :, lanes] = res.astype(BF16)
                else:
                    perm_ref[slot] = res
                    for r in range(dil):
                        ref[0, r, :, lanes] = perm_ref[slot, pl.ds(r, n, stride=dil), :].astype(BF16)
                    slot += 1

    for c, r in enumerate(project(C_WIDTH, C_WIDTH, True, None)):
        ck_ref[0, :, c * LANES:(c + 1) * LANES] = r.astype(BF16)

    qt = _mm_nt(wqt_ref[...], xs_ref[...])
    pieces = []
    for hd in range(C_WIDTH // HEAD_DIM):
        x1 = qt[hd * HEAD_DIM:hd * HEAD_DIM + half]
        x2 = qt[hd * HEAD_DIM + half:hd * HEAD_DIM + ROPE_DIM]
        pieces += [x1 * cos_t - x2 * sin_t, x2 * cos_t + x1 * sin_t, qt[hd * HEAD_DIM + ROPE_DIM:(hd + 1) * HEAD_DIM]]
    qt = jnp.concatenate(pieces, axis=0) * (HEAD_DIM ** -0.5 * math.log2(math.e))
    for hp in range(C_HEADS):
        cqt_ref[0, hp, 0] = qt[hp * LANES:(hp + 1) * LANES].astype(BF16)

    vt = _mm_nt(wvt_ref[...], xs_ref[...])
    for hp in range(C_HEADS):
        for j in range(tm // ATT_K):
            cvt_ref[0, hp, j] = vt[hp * LANES:(hp + 1) * LANES, j * ATT_K:(j + 1) * ATT_K].astype(BF16)


def _qkv_proj(h, positions, gain, inv_col, wq_t, wv_t, w_nat):
    bn, s, _ = h.shape
    tm = TOKEN_TILE
    main = pl.BlockSpec((1, tm, D_MODEL), lambda b, i: (b, i, 0))
    pos_spec = pl.BlockSpec((1, 1, tm), lambda b, i: (b, 0, i))
    out_shapes = [jax.ShapeDtypeStruct((bn, C_HEADS, s // ATT_Q, LANES, ATT_Q), BF16),
                  jax.ShapeDtypeStruct((bn, s, C_WIDTH), BF16),
                  jax.ShapeDtypeStruct((bn, C_HEADS, s // ATT_K, LANES, ATT_K), BF16)]
    out_specs = [pl.BlockSpec((1, C_HEADS, tm // ATT_Q, LANES, ATT_Q), lambda b, i: (b, 0, i, 0, 0)),
                 pl.BlockSpec((1, tm, C_WIDTH), lambda b, i: (b, i, 0)),
                 pl.BlockSpec((1, C_HEADS, tm // ATT_K, LANES, ATT_K), lambda b, i: (b, 0, i, 0, 0))]
    n_perm = 0
    for _, dil in D_PATTERNS:
        out_shapes += [jax.ShapeDtypeStruct((bn, dil, s // dil, D_GROUP_WIDTH), BF16)] * 3
        out_specs += [pl.BlockSpec((1, dil, tm // dil, D_GROUP_WIDTH), lambda b, i: (b, 0, i, 0))] * 3
        n_perm += 0 if dil == 1 else 3 * D_GROUP_WIDTH // LANES
    return pl.pallas_call(
        _qkv_kernel,
        grid=(bn, s // tm),
        in_specs=[main, pos_spec, _const_spec((1, D_MODEL)), _const_spec(inv_col.shape), _const_spec(wq_t.shape),
                  _const_spec(wv_t.shape), _const_spec(w_nat.shape)],
        out_specs=out_specs,
        out_shape=out_shapes,
        scratch_shapes=[pltpu.VMEM((tm, D_MODEL), BF16), pltpu.VMEM((3, tm, LANES), F32),
                        pltpu.VMEM((n_perm, tm, LANES), F32)],
        compiler_params=_params(("parallel", "parallel")),
        name="qkv_proj",
    )(h, positions.reshape(bn, 1, s), gain, inv_col, wq_t, wv_t, w_nat)


def _diff_kernel(qt_ref, k_ref, vt_ref, lam_ref, sg_ref, o_ref, qc_ref, s_ref, cm_ref, e_ref, m_ref, acc_ref):
    tq, tk = ATT_Q, ATT_K
    streams = range(qt_ref.shape[1])
    n_strips = 2 * tq // LANES
    all_strips = tuple(range(n_strips))
    late = tuple(j for j in all_strips if (j * LANES) % tq >= tk)
    ones = jnp.ones((V_ROWS - LANES, tk), BF16)
    row = lax.broadcasted_iota(jnp.int32, (LANES, tq), 0)
    lam_rows = lam_ref[...]
    lam = (jnp.exp(jnp.sum(lam_rows[0:1] * lam_rows[1:2], axis=-1, keepdims=True))
           - jnp.exp(jnp.sum(lam_rows[2:3] * lam_rows[3:4], axis=-1, keepdims=True)) + LAMBDA_INIT)
    out_gain = sg_ref[...] * (1.0 - LAMBDA_INIT)

    def qk(st, kb, buf, strips=all_strips):
        start = pl.multiple_of(kb * tk, tk)
        rhs = qc_ref[st] if strips == all_strips else jnp.concatenate(
            [qc_ref[st, :, j * LANES:(j + 1) * LANES] for j in strips], axis=1)
        s = _mm(k_ref[0, pl.ds(start, tk), st * LANES:(st + 1) * LANES], rhs)
        for idx, j in enumerate(strips):
            strip = s[:, idx * LANES:(idx + 1) * LANES]
            s_ref[st, buf, j, 0:tk, :] = strip
            cm_ref[st, buf, :, j * LANES:(j + 1) * LANES] = jnp.max(strip, axis=0, keepdims=True)

    def soft(st, buf, kb, strips=all_strips, key_off=None):
        alphas = []
        for j in strips:
            cols = slice(j * LANES, (j + 1) * LANES)
            s = s_ref[st, buf, j, 0:tk, :]
            q_rel = (j * LANES) % tq
            col_max = cm_ref[st, buf, :, cols]
            if key_off is not None and q_rel < key_off + tk:
                key = key_off + lax.broadcasted_iota(jnp.int32, (tk, LANES), 0)
                qry = q_rel + lax.broadcasted_iota(jnp.int32, (tk, LANES), 1)
                s = jnp.where(key <= qry, s, NEG_BIG)
                col_max = jnp.max(s, axis=0, keepdims=True)
            m_old = m_ref[st, :, cols]
            m_new = jnp.maximum(m_old, col_max)
            alphas.append(jnp.exp2(m_old - m_new))
            m_ref[st, :, cols] = m_new
            e_ref[st, j, 0:tk, :] = jnp.exp2(s - m_new).astype(BF16)
        lhs = jnp.concatenate([vt_ref[0, st, kb], ones], axis=0)
        pv = _mm(lhs, jnp.concatenate([e_ref[st, j, 0:tk, :] for j in strips], axis=1))
        if strips == all_strips:
            acc_ref[st] = acc_ref[st] * jnp.concatenate(alphas, axis=1) + pv
        else:
            for idx, j in enumerate(strips):
                cols = slice(j * LANES, (j + 1) * LANES)
                acc_ref[st, :, cols] = acc_ref[st, :, cols] * alphas[idx] + pv[:, idx * LANES:(idx + 1) * LANES]

    def load_queries(st, qj):
        qt = qt_ref[0, st, qj]
        zero = jnp.zeros_like(qt)
        qc_ref[st, :, :tq] = jnp.where(row < HEAD_DIM, qt, zero)
        qc_ref[st, :, tq:] = jnp.where(row >= HEAD_DIM, qt, zero)

    def finish(st, qj):
        inv = 1.0 / acc_ref[st, LANES:LANES + 1, :]
        ot = acc_ref[st, 0:LANES, :tq] * inv[:, :tq] - lam * (acc_ref[st, 0:LANES, tq:] * inv[:, tq:])
        ot = ot * lax.rsqrt(jnp.mean(ot * ot, axis=0, keepdims=True) + EPS) * out_gain
        o_ref[0, pl.ds(pl.multiple_of(qj * tq, tq), tq), st * LANES:(st + 1) * LANES] = ot.T.astype(BF16)

    n_blocks = qt_ref.shape[2]
    for st in streams:
        load_queries(st, 0)
        qk(st, 0, 0)

    def query_block(qj, _):
        m_ref[...] = jnp.full(m_ref.shape, NEG_BIG, F32)
        acc_ref[...] = jnp.zeros(acc_ref.shape, F32)

        def pair(it, _):
            ns = len(streams)
            for st in streams:
                qk(st, 2 * it + 1, 1)
            for st in range(ns + 1):
                if st < ns:
                    soft(st, 0, 2 * it)
                    qk(st, 2 * it + 2, 0)
                if st >= 1:
                    soft(st - 1, 1, 2 * it + 1)
            return 0

        lax.fori_loop(0, qj, pair, 0)
        for st in streams:
            qk(st, 2 * qj + 1, 1, late)
            soft(st, 0, 2 * qj, key_off=0)
            load_queries(st, jnp.minimum(qj + 1, n_blocks - 1))
            qk(st, 0, 0)
            soft(st, 1, 2 * qj + 1, late, key_off=tk)
            finish(st, qj)
        return 0

    lax.fori_loop(0, n_blocks, query_block, 0)


def _diff_attention(cqt, ck, cvt, lam_rows, subln_gain_col):
    bn, s, _ = ck.shape
    tq, tk, ns = ATT_Q, ATT_K, ATT_STREAMS
    return pl.pallas_call(
        _diff_kernel,
        grid=(bn, C_HEADS // ns),
        in_specs=[pl.BlockSpec((1, ns, s // tq, LANES, tq), lambda b, h: (b, h, 0, 0, 0)),
                  pl.BlockSpec((1, s, ns * LANES), lambda b, h: (b, 0, h)),
                  pl.BlockSpec((1, ns, s // tk, LANES, tk), lambda b, h: (b, h, 0, 0, 0)),
                  pl.BlockSpec(lam_rows.shape, lambda b, h: (0, 0)),
                  pl.BlockSpec((LANES, 1), lambda b, h: (0, 0))],
        out_specs=pl.BlockSpec((1, s, ns * LANES), lambda b, h: (b, 0, h)),
        out_shape=jax.ShapeDtypeStruct((bn, s, C_WIDTH), BF16),
        scratch_shapes=[pltpu.VMEM((ns, LANES, 2 * tq), BF16),
                        pltpu.VMEM((ns, 2, 2 * tq // LANES, tk + STRIP_PAD, LANES), F32),
                        pltpu.VMEM((ns, 2, 1, 2 * tq), F32),
                        pltpu.VMEM((ns, 2 * tq // LANES, tk + 2 * STRIP_PAD, LANES), BF16), pltpu.VMEM((ns, 1, 2 * tq), F32),
                        pltpu.VMEM((ns, V_ROWS, 2 * tq), F32)],
        compiler_params=_params(("parallel", "parallel")),
        name="diff_attention",
    )(cqt, ck, cvt, lam_rows, subln_gain_col)


def _dil_kernel(q0, k0, v0, q1, k1, v1, q2, k2, v2, o_ref, os_ref, ls_ref):
    nk = DIL_BLOCK
    s_len = o_ref.shape[1]
    lane = lax.broadcasted_iota(jnp.int32, (nk, LANES), 1)
    first_head = lane < HEAD_DIM
    row = lax.broadcasted_iota(jnp.int32, (2 * nk, 2 * nk), 0) % nk
    col = lax.broadcasted_iota(jnp.int32, (2 * nk, 2 * nk), 1)
    band = ((col < nk) & (col >= row)) | ((col >= nk) & (col - nk <= row))
    ones = jnp.ones((2 * nk, LANES), BF16)

    for grp, ((_, dil), q_ref, k_ref, v_ref) in enumerate(zip(D_PATTERNS, (q0, q1, q2), (k0, k1, k2), (v0, v1, v2))):
        nb = s_len // dil // nk
        per_res = min(nb, DIL_BATCH)
        n_res = DIL_BATCH // per_res
        chunks = nb // per_res

        def batch(t, _, dil=dil, per_res=per_res, n_res=n_res, chunks=chunks, grp=grp,
                  q_ref=q_ref, k_ref=k_ref, v_ref=v_ref):
            n0 = (t % chunks) * per_res
            base = pl.multiple_of(n0 * nk, nk)
            prev = pl.multiple_of(jnp.maximum(n0 - 1, 0) * nk, nk)
            for rr in range(n_res):
                r = (t // chunks) * n_res + rr
                q_all = q_ref[0, r, pl.ds(base, per_res * nk), :]
                k_all = jnp.concatenate([k_ref[0, r, pl.ds(prev, nk), :], k_ref[0, r, pl.ds(base, per_res * nk), :]], axis=0)
                v_all = jnp.concatenate([v_ref[0, r, pl.ds(prev, nk), :], v_ref[0, r, pl.ds(base, per_res * nk), :]], axis=0)
                for i in range(per_res):
                    q = q_all[i * nk:(i + 1) * nk]
                    k = k_all[i * nk:(i + 2) * nk]
                    v = jnp.concatenate([v_all[i * nk:(i + 2) * nk], ones], axis=1)
                    mask = band
                    if i == 0:
                        mask = band & (col >= jnp.where(n0 == 0, nk, 0))
                    zero = jnp.zeros_like(q)
                    q2 = jnp.concatenate([jnp.where(first_head, q, zero), jnp.where(first_head, zero, q)], axis=0)
                    s = jnp.where(mask, _mm_nt(q2, k), NEG_BIG)
                    m = jnp.max(s, axis=-1, keepdims=True)
                    pv = _mm(jnp.exp2(s - m).astype(BF16), v)
                    num = jnp.where(first_head, pv[:nk, :LANES], pv[nk:, :LANES])
                    den = jnp.where(first_head, pv[:nk, LANES:], pv[nk:, LANES:])
                    top = jnp.where(first_head, jnp.broadcast_to(m[:nk], (nk, LANES)), jnp.broadcast_to(m[nk:], (nk, LANES)))
                    rows = pl.ds((n0 + i) * (nk * dil) + r, nk, stride=dil)
                    os_ref[grp, rows, :] = num / den
                    ls_ref[grp, rows, :] = top + jnp.log2(den)
            return 0

        lax.fori_loop(0, (dil // n_res) * chunks, batch, 0)

    tile = 512
    for t in range(s_len // tile):
        rows = slice(t * tile, (t + 1) * tile)
        lse = [ls_ref[grp, rows, :] for grp in range(3)]
        top = jnp.maximum(jnp.maximum(lse[0], lse[1]), lse[2])
        wts = [jnp.exp2(x - top) for x in lse]
        num = wts[0] * os_ref[0, rows, :] + wts[1] * os_ref[1, rows, :] + wts[2] * os_ref[2, rows, :]
        o_ref[0, rows, :] = (num / (wts[0] + wts[1] + wts[2])).astype(BF16)


def _dilated_attention(dqkv):
    bn = dqkv[0].shape[0]
    s = dqkv[0].shape[2]
    specs = []
    for (_, dil) in D_PATTERNS:
        specs += [pl.BlockSpec((1, dil, s // dil, LANES), lambda b, hp: (b, 0, 0, hp))] * 3
    return pl.pallas_call(
        _dil_kernel,
        grid=(bn, D_GROUP_WIDTH // LANES),
        in_specs=specs,
        out_specs=pl.BlockSpec((1, s, LANES), lambda b, hp: (b, 0, hp)),
        out_shape=jax.ShapeDtypeStruct((bn, s, D_GROUP_WIDTH), BF16),
        scratch_shapes=[pltpu.VMEM((3, s, LANES), F32), pltpu.VMEM((3, s, LANES), F32)],
        compiler_params=_params(("parallel", "parallel")),
        name="dilated_attention",
    )(*dqkv)


def kernel(x, positions, norm_mix, norm_ffn, final_norm, even_w_in, gmlp_v_gain, gmlp_w_s, gmlp_b_s, pool_w,
           pool_scale, even_w_out, odd_w_in, lambda_q1, lambda_k1, lambda_q2, lambda_k2, subln_gain, odd_w_out,
           ffn_w_up, ffn_conv_w, ffn_conv_b, ffn_w_down):
    row = lambda a: a.reshape(1, -1)
    bf = lambda a: a.astype(BF16)
    depth = norm_ffn.shape[0]

    later = [ffn_w_up.reshape(depth * D_MODEL, 2 * D_FF), ffn_w_down.reshape(depth * D_FF, D_MODEL),
             odd_w_in[0], odd_w_out[0], even_w_out[0]]
    y, (w_up, w_down, w_in, w_out_odd, w_out_even) = _even_mixer(
        x, row(norm_mix[0]), bf(even_w_in[0]), row(gmlp_v_gain[0]), gmlp_w_s[0], gmlp_b_s[0].T,
        bf(pool_w[0]), row(pool_scale[0]), later)
    ffn_params = (norm_ffn.reshape(depth, 1, D_MODEL), w_up.reshape(depth, D_MODEL, 2 * D_FF),
                  ffn_conv_w.reshape(depth, 3, D_FF), ffn_conv_b.reshape(depth, 1, D_FF),
                  w_down.reshape(depth, D_FF, D_MODEL), row(final_norm))
    h = _mix_ffn(x, y, y, (0, 2), (1, 2), w_out_even, *ffn_params, layer=0, final_norm=False)

    inv = ROPE_THETA ** (-jnp.arange(0, ROPE_DIM, 2, dtype=F32) / ROPE_DIM)
    qkv = _qkv_proj(h, positions, row(norm_mix[1]), inv.reshape(-1, 1), w_in[:, :C_WIDTH].T,
                    w_in[:, 2 * C_WIDTH:3 * C_WIDTH].T, w_in)
    lam_rows = jnp.stack([lambda_q1[0], lambda_k1[0], lambda_q2[0], lambda_k2[0]])
    yc = _diff_attention(qkv[0], qkv[1], qkv[2], lam_rows, subln_gain[0].reshape(-1, 1))
    yd = _dilated_attention(qkv[3:])
    return _mix_ffn(h, yc, yd, (0, 1), (0, 1), w_out_odd, *ffn_params, layer=1, final_norm=True)
```

```python
import functools
import math

import jax
import jax.numpy as jnp
from jax import lax
from jax.experimental import pallas as pl
from jax.experimental.pallas import tpu as pltpu

F32 = jnp.float32
BF16 = jnp.bfloat16

D_MODEL = 1024
A_WIDTH = 512
A_GROUPS = 4
CHUNK = 128
B_WIDTH = 512
POOL_WINDOWS = (2, 4, 8, 16)
HEAD_DIM = 64
C_HEADS = 6
C_WIDTH = 768
D_PATTERNS = ((128, 1), (512, 4), (2048, 16))
D_GROUP_WIDTH = 256
ROPE_THETA = 500000.0
ROPE_DIM = 16
D_FF = 2816
EPS = 1e-6
LAMBDA_INIT = 0.8 - 0.6 * math.exp(-0.3 * 1)

LANES = 128
HALO = 16
TOKEN_TILE = 512
FF_CHUNK = 256
ATT_Q = 512
ATT_K = 256
MXU_COLS = 256
ATT_STREAMS = 3
STRIP_PAD = 8
V_ROWS = 144
DIL_BATCH = 16
DIL_BLOCK = 128
VMEM_LIMIT = 56 * 1024 * 1024
NEG_BIG = -1e30


def _rms(x, g):
    return x * lax.rsqrt(jnp.mean(x * x, axis=-1, keepdims=True) + EPS) * g


def _gelu(x):
    c = math.sqrt(2.0 / math.pi)
    return 0.5 * x * (1.0 + jnp.tanh(c * (x + 0.044715 * (x * x * x))))


def _mm(a, b):
    return jnp.dot(a, b, preferred_element_type=F32)


def _mm_nt(a, b):
    return lax.dot_general(a, b, (((1,), (1,)), ((), ())), preferred_element_type=F32)


def _const_spec(shape):
    nd = len(shape)
    return pl.BlockSpec(shape, lambda *_: (0,) * nd, pipeline_mode=pl.Buffered(1))


def _layer_spec(shape, layer):
    nd = len(shape)
    return pl.BlockSpec((None,) + tuple(shape[1:]), lambda *_: (layer,) + (0,) * (nd - 1),
                        pipeline_mode=pl.Buffered(1))


def _params(semantics):
    return pltpu.CompilerParams(dimension_semantics=semantics, vmem_limit_bytes=VMEM_LIMIT)


def _even_kernel(h_ref, halo_ref, g_ref, win_ref, vg_ref, ws_ref, bs_ref, pw_ref, ps_ref, *refs):
    n_cast = (len(refs) - 3) // 2
    cast_in, y_ref, cast_out = refs[:n_cast], refs[n_cast], refs[n_cast + 1:2 * n_cast + 1]
    xs_ref, pb_ref = refs[2 * n_cast + 1:]
    for src, dst in zip(cast_in, cast_out):
        dst[...] = src[...].astype(BF16)
    tm = h_ref.shape[1]
    i = pl.program_id(1)
    g = g_ref[...]
    xh = jnp.where(i == 0, 0.0, halo_ref[0])
    xs_ref[0:HALO, :] = _rms(xh, g).astype(BF16)
    xs_ref[HALO:, :] = _rms(h_ref[0], g).astype(BF16)

    z = _gelu(_mm(xs_ref[HALO:, :], win_ref[:, :2 * A_WIDTH]))
    u = z[:, :A_WIDTH]
    v = _rms(z[:, A_WIDTH:], vg_ref[...]).astype(BF16)
    n_chunks = tm // CHUNK
    row = lax.broadcasted_iota(jnp.int32, (CHUNK, CHUNK), 0)
    col = lax.broadcasted_iota(jnp.int32, (CHUNK, CHUNK), 1)
    for grp in range(A_GROUPS):
        lanes = slice(grp * LANES, (grp + 1) * LANES)
        w = jnp.where(row >= col, ws_ref[grp], 0.0).astype(BF16)
        rhs = jnp.concatenate([v[c * CHUNK:(c + 1) * CHUNK, lanes] for c in range(n_chunks)], axis=1)
        mixed = _mm(w, rhs) + bs_ref[:, grp:grp + 1]
        for c in range(n_chunks):
            rows = slice(c * CHUNK, (c + 1) * CHUNK)
            y_ref[0, rows, lanes] = (u[rows, lanes] * mixed[:, c * CHUNK:(c + 1) * CHUNK]).astype(BF16)

    pb_ref[...] = _mm(xs_ref[...], win_ref[:, 2 * A_WIDTH:])
    pos = i * tm + lax.broadcasted_iota(jnp.int32, (tm, 1), 0)
    for grp, window in enumerate(POOL_WINDOWS):
        lanes = slice(grp * LANES, (grp + 1) * LANES)
        acc = pb_ref[:, lanes]
        span = 1
        while span < window:
            acc = acc + pltpu.roll(acc, span, 0)
            span *= 2
        inv_count = 1.0 / jnp.minimum(pos + 1, window).astype(F32)
        pooled = (acc[HALO:] * inv_count - pb_ref[HALO:, lanes]).astype(BF16)
        yb = _mm(pooled, pw_ref[grp]) * ps_ref[:, lanes]
        y_ref[0, :, A_WIDTH + grp * LANES:A_WIDTH + (grp + 1) * LANES] = yb.astype(BF16)


def _even_mixer(h, gain, w_in, v_gain, w_s, b_s, pool_w, pool_scale, later_weights):
    bn, s, _ = h.shape
    tm = TOKEN_TILE
    n_steps = bn * (s // tm)
    main = pl.BlockSpec((1, tm, D_MODEL), lambda b, i: (b, i, 0))
    halo = pl.BlockSpec((1, HALO, D_MODEL), lambda b, i: (b, jnp.maximum(i * (tm // HALO) - 1, 0), 0))
    slabs = [pl.BlockSpec((w.shape[0] // n_steps, w.shape[1]), lambda b, i: (b * (s // tm) + i, 0))
             for w in later_weights]
    outs = pl.pallas_call(
        _even_kernel,
        grid=(bn, s // tm),
        in_specs=[main, halo, _const_spec((1, D_MODEL)), _const_spec(w_in.shape), _const_spec((1, A_WIDTH)),
                  _const_spec(w_s.shape), _const_spec(b_s.shape), _const_spec(pool_w.shape),
                  _const_spec((1, B_WIDTH))] + slabs,
        out_specs=[main] + slabs,
        out_shape=[jax.ShapeDtypeStruct(h.shape, BF16)] + [jax.ShapeDtypeStruct(w.shape, BF16) for w in later_weights],
        scratch_shapes=[pltpu.VMEM((tm + HALO, D_MODEL), BF16), pltpu.VMEM((tm + HALO, B_WIDTH), F32)],
        compiler_params=_params(("parallel", "parallel")),
        name="even_mixer",
    )(h, h, gain, w_in, v_gain, w_s, b_s, pool_w, pool_scale, *later_weights)
    return outs[0], outs[1:]


def _ffn_kernel(h_ref, ya_ref, yb_ref, wo_ref, g_ref, wup_ref, cw_ref, cb_ref, wdown_ref, fg_ref, o_ref,
                xs_ref, a_ref, hg_ref, *, final_norm):
    tm = h_ref.shape[1]
    wa = ya_ref.shape[2]
    i = pl.program_id(1)

    @pl.when(i == 0)
    def _():
        xs_ref[0:HALO, :] = jnp.zeros((HALO, D_MODEL), BF16)

    @pl.when(i > 0)
    def _():
        xs_ref[0:HALO, :] = xs_ref[tm:tm + HALO, :]

    hm = h_ref[0] + _mm(ya_ref[0], wo_ref[0:wa, :]) + _mm(yb_ref[0], wo_ref[wa:, :])
    xs_ref[HALO:, :] = _rms(hm, g_ref[...]).astype(BF16)
    for c in range(D_FF // FF_CHUNK):
        cols = slice(c * FF_CHUNK, (c + 1) * FF_CHUNK)
        a_ref[...] = _mm(xs_ref[...], wup_ref[:, cols])
        gate = _mm(xs_ref[HALO:, :], wup_ref[:, D_FF + c * FF_CHUNK:D_FF + (c + 1) * FF_CHUNK])
        conv = (a_ref[pl.ds(HALO - 2, tm), :] * cw_ref[0:1, cols]
                + a_ref[pl.ds(HALO - 1, tm), :] * cw_ref[1:2, cols]
                + a_ref[pl.ds(HALO, tm), :] * cw_ref[2:3, cols]
                + cb_ref[:, cols])
        hg_ref[:, cols] = (_gelu(conv) * gate).astype(BF16)
    y = hm + _mm(hg_ref[...], wdown_ref[...])
    if final_norm:
        y = _rms(y, fg_ref[...])
    o_ref[0] = y


def _mix_ffn(h, ya, yb, col_a, col_b, w_out, gains, w_up, conv_w, conv_b, w_down, final_gain, layer, final_norm):
    bn, s, _ = h.shape
    tm = TOKEN_TILE
    wa, wb = ya.shape[2] // (col_a[1]), yb.shape[2] // (col_b[1])
    main = pl.BlockSpec((1, tm, D_MODEL), lambda b, i: (b, i, 0))
    return pl.pallas_call(
        functools.partial(_ffn_kernel, final_norm=final_norm),
        grid=(bn, s // tm),
        in_specs=[main,
                  pl.BlockSpec((1, tm, wa), lambda b, i: (b, i, col_a[0])),
                  pl.BlockSpec((1, tm, wb), lambda b, i: (b, i, col_b[0])),
                  _const_spec(w_out.shape), _layer_spec(gains.shape, layer), _layer_spec(w_up.shape, layer),
                  _layer_spec(conv_w.shape, layer), _layer_spec(conv_b.shape, layer),
                  _layer_spec(w_down.shape, layer), _const_spec((1, D_MODEL))],
        out_specs=main,
        out_shape=jax.ShapeDtypeStruct(h.shape, F32),
        scratch_shapes=[pltpu.VMEM((tm + HALO, D_MODEL), BF16), pltpu.VMEM((tm + HALO, FF_CHUNK), F32),
                        pltpu.VMEM((tm, D_FF), BF16)],
        compiler_params=_params(("parallel", "arbitrary")),
        name="mix_ffn",
    )(h, ya, yb, w_out, gains, w_up, conv_w, conv_b, w_down, final_gain)


def _rope(x, tab):
    c, s_up, s_lo = tab
    return x * c + pltpu.roll(x, ROPE_DIM // 2, 1) * s_up + pltpu.roll(x, LANES - ROPE_DIM // 2, 1) * s_lo


def _qkv_kernel(h_ref, pos_ref, g_ref, inv_ref, wqt_ref, wvt_ref, wn_ref,
                cqt_ref, ck_ref, cvt_ref, dq0_ref, dk0_ref, dv0_ref, dq1_ref, dk1_ref, dv1_ref,
                dq2_ref, dk2_ref, dv2_ref, xs_ref, tab_ref, perm_ref):
    tm = h_ref.shape[1]
    half = ROPE_DIM // 2
    xs_ref[...] = _rms(h_ref[0], g_ref[...]).astype(BF16)

    ang = inv_ref[...] * pos_ref[0].astype(F32)
    cos_t, sin_t = jnp.cos(ang), jnp.sin(ang)
    rest = HEAD_DIM - ROPE_DIM
    ones, z_half, z_rest = jnp.ones((rest, tm), F32), jnp.zeros((half, tm), F32), jnp.zeros((rest, tm), F32)
    tab_ref[0] = jnp.concatenate([cos_t, cos_t, ones] * 2, axis=0).T
    tab_ref[1] = jnp.concatenate([z_half, sin_t, z_rest] * 2, axis=0).T
    tab_ref[2] = jnp.concatenate([-sin_t, z_half, z_rest] * 2, axis=0).T

    tabs = (tab_ref[0], tab_ref[1], tab_ref[2])

    def project(col0, width, rope, scale):
        outs = []
        for c in range(width // MXU_COLS):
            r = _mm(xs_ref[...], wn_ref[:, col0 + c * MXU_COLS:col0 + (c + 1) * MXU_COLS])
            for x in (r[:, :LANES], r[:, LANES:]):
                if rope:
                    x = _rope(x, tabs)
                if scale is not None:
                    x = x * scale
                outs.append(x)
        return outs

    dscale = HEAD_DIM ** -0.5 * math.log2(math.e)
    gw = D_GROUP_WIDTH
    slot = 0
    group_refs = ((dq0_ref, dk0_ref, dv0_ref), (dq1_ref, dk1_ref, dv1_ref), (dq2_ref, dk2_ref, dv2_ref))
    for grp in reversed(range(len(D_PATTERNS))):
        dil = D_PATTERNS[grp][1]
        n = tm // dil
        for part in (2, 0, 1):
            ref, rope, scale = group_refs[grp][part], part < 2, dscale if part == 0 else None
            col0 = 3 * C_WIDTH + part * len(D_PATTERNS) * gw + grp * gw
            for c, res in enumerate(project(col0, gw, rope, scale)):
                lanes = slice(c * LANES, (c + 1) * LANES)
                if dil == 1:
                    ref[0, 0, :, lanes] = res.astype(BF16)
                else:
                    perm_ref[slot] = res
                    for r in range(dil):
                        ref[0, r, :, lanes] = perm_ref[slot, pl.ds(r, n, stride=dil), :].astype(BF16)
                    slot += 1

    for c, r in enumerate(project(C_WIDTH, C_WIDTH, True, None)):
        ck_ref[0, :, c * LANES:(c + 1) * LANES] = r.astype(BF16)

    qt = _mm_nt(wqt_ref[...], xs_ref[...])
    pieces = []
    for hd in range(C_WIDTH // HEAD_DIM):
        x1 = qt[hd * HEAD_DIM:hd * HEAD_DIM + half]
        x2 = qt[hd * HEAD_DIM + half:hd * HEAD_DIM + ROPE_DIM]
        pieces += [x1 * cos_t - x2 * sin_t, x2 * cos_t + x1 * sin_t, qt[hd * HEAD_DIM + ROPE_DIM:(hd + 1) * HEAD_DIM]]
    qt = jnp.concatenate(pieces, axis=0) * (HEAD_DIM ** -0.5 * math.log2(math.e))
    for hp in range(C_HEADS):
        cqt_ref[0, hp, 0] = qt[hp * LANES:(hp + 1) * LANES].astype(BF16)

    vt = _mm_nt(wvt_ref[...], xs_ref[...])
    for hp in range(C_HEADS):
        for j in range(tm // ATT_K):
            cvt_ref[0, hp, j] = vt[hp * LANES:(hp + 1) * LANES, j * ATT_K:(j + 1) * ATT_K].astype(BF16)


def _qkv_proj(h, positions, gain, inv_col, wq_t, wv_t, w_nat):
    bn, s, _ = h.shape
    tm = TOKEN_TILE
    main = pl.BlockSpec((1, tm, D_MODEL), lambda b, i: (b, i, 0))
    pos_spec = pl.BlockSpec((1, 1, tm), lambda b, i: (b, 0, i))
    out_shapes = [jax.ShapeDtypeStruct((bn, C_HEADS, s // ATT_Q, LANES, ATT_Q), BF16),
                  jax.ShapeDtypeStruct((bn, s, C_WIDTH), BF16),
                  jax.ShapeDtypeStruct((bn, C_HEADS, s // ATT_K, LANES, ATT_K), BF16)]
    out_specs = [pl.BlockSpec((1, C_HEADS, tm // ATT_Q, LANES, ATT_Q), lambda b, i: (b, 0, i, 0, 0)),
                 pl.BlockSpec((1, tm, C_WIDTH), lambda b, i: (b, i, 0)),
                 pl.BlockSpec((1, C_HEADS, tm // ATT_K, LANES, ATT_K), lambda b, i: (b, 0, i, 0, 0))]
    n_perm = 0
    for _, dil in D_PATTERNS:
        out_shapes += [jax.ShapeDtypeStruct((bn, dil, s // dil, D_GROUP_WIDTH), BF16)] * 3
        out_specs += [pl.BlockSpec((1, dil, tm // dil, D_GROUP_WIDTH), lambda b, i: (b, 0, i, 0))] * 3
        n_perm += 0 if dil == 1 else 3 * D_GROUP_WIDTH // LANES
    return pl.pallas_call(
        _qkv_kernel,
        grid=(bn, s // tm),
        in_specs=[main, pos_spec, _const_spec((1, D_MODEL)), _const_spec(inv_col.shape), _const_spec(wq_t.shape),
                  _const_spec(wv_t.shape), _const_spec(w_nat.shape)],
        out_specs=out_specs,
        out_shape=out_shapes,
        scratch_shapes=[pltpu.VMEM((tm, D_MODEL), BF16), pltpu.VMEM((3, tm, LANES), F32),
                        pltpu.VMEM((n_perm, tm, LANES), F32)],
        compiler_params=_params(("parallel", "parallel")),
        name="qkv_proj",
    )(h, positions.reshape(bn, 1, s), gain, inv_col, wq_t, wv_t, w_nat)


def _diff_kernel(qt_ref, k_ref, vt_ref, lam_ref, sg_ref, o_ref, qc_ref, s_ref, cm_ref, e_ref, m_ref, acc_ref):
    tq, tk = ATT_Q, ATT_K
    streams = range(qt_ref.shape[1])
    n_strips = 2 * tq // LANES
    all_strips = tuple(range(n_strips))
    late = tuple(j for j in all_strips if (j * LANES) % tq >= tk)
    ones = jnp.ones((V_ROWS - LANES, tk), BF16)
    row = lax.broadcasted_iota(jnp.int32, (LANES, tq), 0)
    lam_rows = lam_ref[...]
    lam = (jnp.exp(jnp.sum(lam_rows[0:1] * lam_rows[1:2], axis=-1, keepdims=True))
           - jnp.exp(jnp.sum(lam_rows[2:3] * lam_rows[3:4], axis=-1, keepdims=True)) + LAMBDA_INIT)
    out_gain = sg_ref[...] * (1.0 - LAMBDA_INIT)

    def qk(st, kb, buf, strips=all_strips):
        start = pl.multiple_of(kb * tk, tk)
        rhs = qc_ref[st] if strips == all_strips else jnp.concatenate(
            [qc_ref[st, :, j * LANES:(j + 1) * LANES] for j in strips], axis=1)
        s = _mm(k_ref[0, pl.ds(start, tk), st * LANES:(st + 1) * LANES], rhs)
        for idx, j in enumerate(strips):
            strip = s[:, idx * LANES:(idx + 1) * LANES]
            s_ref[st, buf * n_strips + j, 0:tk, :] = strip
            cm_ref[st, buf, :, j * LANES:(j + 1) * LANES] = jnp.max(strip, axis=0, keepdims=True)

    def soft(st, buf, kb, strips=all_strips, key_off=None):
        alphas = []
        for j in strips:
            cols = slice(j * LANES, (j + 1) * LANES)
            s = s_ref[st, buf * n_strips + j, 0:tk, :]
            q_rel = (j * LANES) % tq
            col_max = cm_ref[st, buf, :, cols]
            if key_off is not None and q_rel < key_off + tk:
                key = key_off + lax.broadcasted_iota(jnp.int32, (tk, LANES), 0)
                qry = q_rel + lax.broadcasted_iota(jnp.int32, (tk, LANES), 1)
                s = jnp.where(key <= qry, s, NEG_BIG)
                col_max = jnp.max(s, axis=0, keepdims=True)
            m_old = m_ref[st, :, cols]
            m_new = jnp.maximum(m_old, col_max)
            alphas.append(jnp.exp2(m_old - m_new))
            m_ref[st, :, cols] = m_new
            e_ref[st, j, 0:tk, :] = jnp.exp2(s - m_new).astype(BF16)
        lhs = jnp.concatenate([vt_ref[0, st, kb], ones], axis=0)
        pv = _mm(lhs, jnp.concatenate([e_ref[st, j, 0:tk, :] for j in strips], axis=1))
        if strips == all_strips:
            acc_ref[st] = acc_ref[st] * jnp.concatenate(alphas, axis=1) + pv
        else:
            for idx, j in enumerate(strips):
                cols = slice(j * LANES, (j + 1) * LANES)
                acc_ref[st, :, cols] = acc_ref[st, :, cols] * alphas[idx] + pv[:, idx * LANES:(idx + 1) * LANES]

    def load_queries(st, qj):
        qt = qt_ref[0, st, qj]
        zero = jnp.zeros_like(qt)
        qc_ref[st, :, :tq] = jnp.where(row < HEAD_DIM, qt, zero)
        qc_ref[st, :, tq:] = jnp.where(row >= HEAD_DIM, qt, zero)

    def finish(st, qj):
        inv = 1.0 / acc_ref[st, LANES:LANES + 1, :]
        ot = acc_ref[st, 0:LANES, :tq] * inv[:, :tq] - lam * (acc_ref[st, 0:LANES, tq:] * inv[:, tq:])
        ot = ot * lax.rsqrt(jnp.mean(ot * ot, axis=0, keepdims=True) + EPS) * out_gain
        o_ref[0, pl.ds(pl.multiple_of(qj * tq, tq), tq), st * LANES:(st + 1) * LANES] = ot.T.astype(BF16)

    n_blocks = qt_ref.shape[2]
    for st in streams:
        load_queries(st, 0)
        qk(st, 0, 0)

    def query_block(qj, _):
        m_ref[...] = jnp.full(m_ref.shape, NEG_BIG, F32)
        acc_ref[...] = jnp.zeros(acc_ref.shape, F32)

        def pair(it, _):
            ns = len(streams)
            for st in streams:
                qk(st, 2 * it + 1, 1)
            for st in range(ns + 1):
                if st < ns:
                    soft(st, 0, 2 * it)
                    qk(st, 2 * it + 2, 0)
                if st >= 1:
                    soft(st - 1, 1, 2 * it + 1)
            return 0

        lax.fori_loop(0, qj, pair, 0)
        for st in streams:
            qk(st, 2 * qj + 1, 1, late)
            soft(st, 0, 2 * qj, key_off=0)
            load_queries(st, jnp.minimum(qj + 1, n_blocks - 1))
            qk(st, 0, 0)
            soft(st, 1, 2 * qj + 1, late, key_off=tk)
            finish(st, qj)
        return 0

    lax.fori_loop(0, n_blocks, query_block, 0)


def _diff_attention(cqt, ck, cvt, lam_rows, subln_gain_col):
    bn, s, _ = ck.shape
    tq, tk, ns = ATT_Q, ATT_K, ATT_STREAMS
    return pl.pallas_call(
        _diff_kernel,
        grid=(bn, C_HEADS // ns),
        in_specs=[pl.BlockSpec((1, ns, s // tq, LANES, tq), lambda b, h: (b, h, 0, 0, 0)),
                  pl.BlockSpec((1, s, ns * LANES), lambda b, h: (b, 0, h)),
                  pl.BlockSpec((1, ns, s // tk, LANES, tk), lambda b, h: (b, h, 0, 0, 0)),
                  pl.BlockSpec(lam_rows.shape, lambda b, h: (0, 0)),
                  pl.BlockSpec((LANES, 1), lambda b, h: (0, 0))],
        out_specs=pl.BlockSpec((1, s, ns * LANES), lambda b, h: (b, 0, h)),
        out_shape=jax.ShapeDtypeStruct((bn, s, C_WIDTH), BF16),
        scratch_shapes=[pltpu.VMEM((ns, LANES, 2 * tq), BF16),
                        pltpu.VMEM((ns, 2 * (2 * tq // LANES) + 1, tk + STRIP_PAD, LANES), F32),
                        pltpu.VMEM((ns, 2, 1, 2 * tq), F32),
                        pltpu.VMEM((ns, 2 * tq // LANES + 1, tk + 2 * STRIP_PAD, LANES), BF16),
                        pltpu.VMEM((ns, 1, 2 * tq), F32),
                        pltpu.VMEM((ns, V_ROWS, 2 * tq), F32)],
        compiler_params=_params(("parallel", "parallel")),
        name="diff_attention",
    )(cqt, ck, cvt, lam_rows, subln_gain_col)


def _dil_kernel(q0, k0, v0, q1, k1, v1, q2, k2, v2, o_ref, os_ref, ls_ref):
    nk = DIL_BLOCK
    s_len = o_ref.shape[1]
    lane = lax.broadcasted_iota(jnp.int32, (nk, LANES), 1)
    first_head = lane < HEAD_DIM
    row = lax.broadcasted_iota(jnp.int32, (2 * nk, 2 * nk), 0) % nk
    col = lax.broadcasted_iota(jnp.int32, (2 * nk, 2 * nk), 1)
    band = ((col < nk) & (col >= row)) | ((col >= nk) & (col - nk <= row))
    ones = jnp.ones((2 * nk, LANES), BF16)

    for grp, ((_, dil), q_ref, k_ref, v_ref) in enumerate(zip(D_PATTERNS, (q0, q1, q2), (k0, k1, k2), (v0, v1, v2))):
        nb = s_len // dil // nk
        per_res = min(nb, DIL_BATCH)
        n_res = DIL_BATCH // per_res
        chunks = nb // per_res

        def batch(t, _, dil=dil, per_res=per_res, n_res=n_res, chunks=chunks, grp=grp,
                  q_ref=q_ref, k_ref=k_ref, v_ref=v_ref):
            n0 = (t % chunks) * per_res
            base = pl.multiple_of(n0 * nk, nk)
            prev = pl.multiple_of(jnp.maximum(n0 - 1, 0) * nk, nk)
            for rr in range(n_res):
                r = (t // chunks) * n_res + rr
                q_all = q_ref[0, r, pl.ds(base, per_res * nk), :]
                k_all = jnp.concatenate([k_ref[0, r, pl.ds(prev, nk), :], k_ref[0, r, pl.ds(base, per_res * nk), :]], axis=0)
                v_all = jnp.concatenate([v_ref[0, r, pl.ds(prev, nk), :], v_ref[0, r, pl.ds(base, per_res * nk), :]], axis=0)
                for i in range(per_res):
                    q = q_all[i * nk:(i + 1) * nk]
                    k = k_all[i * nk:(i + 2) * nk]
                    v = jnp.concatenate([v_all[i * nk:(i + 2) * nk], ones], axis=1)
                    mask = band
                    if i == 0:
                        mask = band & (col >= jnp.where(n0 == 0, nk, 0))
                    zero = jnp.zeros_like(q)
                    q2 = jnp.concatenate([jnp.where(first_head, q, zero), jnp.where(first_head, zero, q)], axis=0)
                    s = jnp.where(mask, _mm_nt(q2, k), NEG_BIG)
                    m = jnp.max(s, axis=-1, keepdims=True)
                    pv = _mm(jnp.exp2(s - m).astype(BF16), v)
                    num = jnp.where(first_head, pv[:nk, :LANES], pv[nk:, :LANES])
                    den = jnp.where(first_head, pv[:nk, LANES:], pv[nk:, LANES:])
                    top = jnp.where(first_head, jnp.broadcast_to(m[:nk], (nk, LANES)), jnp.broadcast_to(m[nk:], (nk, LANES)))
                    rows = pl.ds((n0 + i) * (nk * dil) + r, nk, stride=dil)
                    os_ref[grp, rows, :] = num / den
                    ls_ref[grp, rows, :] = top + jnp.log2(den)
            return 0

        lax.fori_loop(0, (dil // n_res) * chunks, batch, 0)

    tile = 512
    for t in range(s_len // tile):
        rows = slice(t * tile, (t + 1) * tile)
        lse = [ls_ref[grp, rows, :] for grp in range(3)]
        top = jnp.maximum(jnp.maximum(lse[0], lse[1]), lse[2])
        wts = [jnp.exp2(x - top) for x in lse]
        num = wts[0] * os_ref[0, rows, :] + wts[1] * os_ref[1, rows, :] + wts[2] * os_ref[2, rows, :]
        o_ref[0, rows, :] = (num / (wts[0] + wts[1] + wts[2])).astype(BF16)


def _dilated_attention(dqkv):
    bn = dqkv[0].shape[0]
    s = dqkv[0].shape[2]
    specs = []
    for (_, dil) in D_PATTERNS:
        specs += [pl.BlockSpec((1, dil, s // dil, LANES), lambda b, hp: (b, 0, 0, hp))] * 3
    return pl.pallas_call(
        _dil_kernel,
        grid=(bn, D_GROUP_WIDTH // LANES),
        in_specs=specs,
        out_specs=pl.BlockSpec((1, s, LANES), lambda b, hp: (b, 0, hp)),
        out_shape=jax.ShapeDtypeStruct((bn, s, D_GROUP_WIDTH), BF16),
        scratch_shapes=[pltpu.VMEM((3, s, LANES), F32), pltpu.VMEM((3, s, LANES), F32)],
        compiler_params=_params(("parallel", "parallel")),
        name="dilated_attention",
    )(*dqkv)


def kernel(x, positions, norm_mix, norm_ffn, final_norm, even_w_in, gmlp_v_gain, gmlp_w_s, gmlp_b_s, pool_w,
           pool_scale, even_w_out, odd_w_in, lambda_q1, lambda_k1, lambda_q2, lambda_k2, subln_gain, odd_w_out,
           ffn_w_up, ffn_conv_w, ffn_conv_b, ffn_w_down):
    row = lambda a: a.reshape(1, -1)
    bf = lambda a: a.astype(BF16)
    depth = norm_ffn.shape[0]

    later = [ffn_w_up.reshape(depth * D_MODEL, 2 * D_FF), ffn_w_down.reshape(depth * D_FF, D_MODEL),
             odd_w_in[0], odd_w_out[0], even_w_out[0]]
    y, (w_up, w_down, w_in, w_out_odd, w_out_even) = _even_mixer(
        x, row(norm_mix[0]), bf(even_w_in[0]), row(gmlp_v_gain[0]), gmlp_w_s[0], gmlp_b_s[0].T,
        bf(pool_w[0]), row(pool_scale[0]), later)
    ffn_params = (norm_ffn.reshape(depth, 1, D_MODEL), w_up.reshape(depth, D_MODEL, 2 * D_FF),
                  ffn_conv_w.reshape(depth, 3, D_FF), ffn_conv_b.reshape(depth, 1, D_FF),
                  w_down.reshape(depth, D_FF, D_MODEL), row(final_norm))
    h = _mix_ffn(x, y, y, (0, 2), (1, 2), w_out_even, *ffn_params, layer=0, final_norm=False)

    inv = ROPE_THETA ** (-jnp.arange(0, ROPE_DIM, 2, dtype=F32) / ROPE_DIM)
    qkv = _qkv_proj(h, positions, row(norm_mix[1]), inv.reshape(-1, 1), w_in[:, :C_WIDTH].T,
                    w_in[:, 2 * C_WIDTH:3 * C_WIDTH].T, w_in)
    lam_rows = jnp.stack([lambda_q1[0], lambda_k1[0], lambda_q2[0], lambda_k2[0]])
    yc = _diff_attention(qkv[0], qkv[1], qkv[2], lam_rows, subln_gain[0].reshape(-1, 1))
    yd = _dilated_attention(qkv[3:])
    return _mix_ffn(h, yc, yd, (0, 1), (0, 1), w_out_odd, *ffn_params, layer=1, final_norm=True)
```

```python
import functools
import math

import jax
import jax.numpy as jnp
from jax import lax
from jax.experimental import pallas as pl
from jax.experimental.pallas import tpu as pltpu

F32 = jnp.float32
BF16 = jnp.bfloat16

D_MODEL = 1024
A_WIDTH = 512
A_GROUPS = 4
CHUNK = 128
B_WIDTH = 512
POOL_WINDOWS = (2, 4, 8, 16)
HEAD_DIM = 64
C_HEADS = 6
C_WIDTH = 768
D_PATTERNS = ((128, 1), (512, 4), (2048, 16))
D_GROUP_WIDTH = 256
ROPE_THETA = 500000.0
ROPE_DIM = 16
D_FF = 2816
EPS = 1e-6
LAMBDA_INIT = 0.8 - 0.6 * math.exp(-0.3 * 1)

LANES = 128
HALO = 16
TOKEN_TILE = 512
FF_CHUNK = 256
ATT_Q = 512
ATT_K = 256
MXU_COLS = 256
ATT_STREAMS = 3
STRIP_PAD = 8
V_ROWS = 144
DIL_BATCH = 16
DIL_BLOCK = 128
VMEM_LIMIT = 56 * 1024 * 1024
NEG_BIG = -1e30


def _rms(x, g):
    return x * lax.rsqrt(jnp.mean(x * x, axis=-1, keepdims=True) + EPS) * g


def _gelu(x):
    c = math.sqrt(2.0 / math.pi)
    return 0.5 * x * (1.0 + jnp.tanh(c * (x + 0.044715 * (x * x * x))))


def _mm(a, b):
    return jnp.dot(a, b, preferred_element_type=F32)


def _mm_nt(a, b):
    return lax.dot_general(a, b, (((1,), (1,)), ((), ())), preferred_element_type=F32)


def _const_spec(shape):
    nd = len(shape)
    return pl.BlockSpec(shape, lambda *_: (0,) * nd, pipeline_mode=pl.Buffered(1))


def _layer_spec(shape, layer):
    nd = len(shape)
    return pl.BlockSpec((None,) + tuple(shape[1:]), lambda *_: (layer,) + (0,) * (nd - 1),
                        pipeline_mode=pl.Buffered(1))


def _params(semantics):
    return pltpu.CompilerParams(dimension_semantics=semantics, vmem_limit_bytes=VMEM_LIMIT)


def _even_kernel(h_ref, halo_ref, g_ref, win_ref, vg_ref, ws_ref, bs_ref, pw_ref, ps_ref, *refs):
    n_cast = (len(refs) - 3) // 2
    cast_in, y_ref, cast_out = refs[:n_cast], refs[n_cast], refs[n_cast + 1:2 * n_cast + 1]
    xs_ref, pb_ref = refs[2 * n_cast + 1:]
    for src, dst in zip(cast_in, cast_out):
        dst[...] = src[...].astype(BF16)
    tm = h_ref.shape[1]
    i = pl.program_id(1)
    g = g_ref[...]
    xh = jnp.where(i == 0, 0.0, halo_ref[0])
    xs_ref[0:HALO, :] = _rms(xh, g).astype(BF16)
    xs_ref[HALO:, :] = _rms(h_ref[0], g).astype(BF16)

    z = _gelu(_mm(xs_ref[HALO:, :], win_ref[:, :2 * A_WIDTH]))
    u = z[:, :A_WIDTH]
    v = _rms(z[:, A_WIDTH:], vg_ref[...]).astype(BF16)
    n_chunks = tm // CHUNK
    row = lax.broadcasted_iota(jnp.int32, (CHUNK, CHUNK), 0)
    col = lax.broadcasted_iota(jnp.int32, (CHUNK, CHUNK), 1)
    for grp in range(A_GROUPS):
        lanes = slice(grp * LANES, (grp + 1) * LANES)
        w = jnp.where(row >= col, ws_ref[grp], 0.0).astype(BF16)
        rhs = jnp.concatenate([v[c * CHUNK:(c + 1) * CHUNK, lanes] for c in range(n_chunks)], axis=1)
        mixed = _mm(w, rhs) + bs_ref[:, grp:grp + 1]
        for c in range(n_chunks):
            rows = slice(c * CHUNK, (c + 1) * CHUNK)
            y_ref[0, rows, lanes] = (u[rows, lanes] * mixed[:, c * CHUNK:(c + 1) * CHUNK]).astype(BF16)

    pb_ref[...] = _mm(xs_ref[...], win_ref[:, 2 * A_WIDTH:])
    pos = i * tm + lax.broadcasted_iota(jnp.int32, (tm, 1), 0)
    for grp, window in enumerate(POOL_WINDOWS):
        lanes = slice(grp * LANES, (grp + 1) * LANES)
        acc = pb_ref[:, lanes]
        span = 1
        while span < window:
            acc = acc + pltpu.roll(acc, span, 0)
            span *= 2
        inv_count = 1.0 / jnp.minimum(pos + 1, window).astype(F32)
        pooled = (acc[HALO:] * inv_count - pb_ref[HALO:, lanes]).astype(BF16)
        yb = _mm(pooled, pw_ref[grp]) * ps_ref[:, lanes]
        y_ref[0, :, A_WIDTH + grp * LANES:A_WIDTH + (grp + 1) * LANES] = yb.astype(BF16)


def _even_mixer(h, gain, w_in, v_gain, w_s, b_s, pool_w, pool_scale, later_weights):
    bn, s, _ = h.shape
    tm = TOKEN_TILE
    n_steps = bn * (s // tm)
    main = pl.BlockSpec((1, tm, D_MODEL), lambda b, i: (b, i, 0))
    halo = pl.BlockSpec((1, HALO, D_MODEL), lambda b, i: (b, jnp.maximum(i * (tm // HALO) - 1, 0), 0))
    slabs = [pl.BlockSpec((w.shape[0] // n_steps, w.shape[1]), lambda b, i: (b * (s // tm) + i, 0))
             for w in later_weights]
    outs = pl.pallas_call(
        _even_kernel,
        grid=(bn, s // tm),
        in_specs=[main, halo, _const_spec((1, D_MODEL)), _const_spec(w_in.shape), _const_spec((1, A_WIDTH)),
                  _const_spec(w_s.shape), _const_spec(b_s.shape), _const_spec(pool_w.shape),
                  _const_spec((1, B_WIDTH))] + slabs,
        out_specs=[main] + slabs,
        out_shape=[jax.ShapeDtypeStruct(h.shape, BF16)] + [jax.ShapeDtypeStruct(w.shape, BF16) for w in later_weights],
        scratch_shapes=[pltpu.VMEM((tm + HALO, D_MODEL), BF16), pltpu.VMEM((tm + HALO, B_WIDTH), F32)],
        compiler_params=_params(("parallel", "parallel")),
        name="even_mixer",
    )(h, h, gain, w_in, v_gain, w_s, b_s, pool_w, pool_scale, *later_weights)
    return outs[0], outs[1:]


def _ffn_kernel(h_ref, ya_ref, yb_ref, wo_ref, g_ref, wup_ref, cw_ref, cb_ref, wdown_ref, fg_ref, o_ref,
                xs_ref, a_ref, hg_ref, *, final_norm):
    tm = h_ref.shape[1]
    wa = ya_ref.shape[2]
    i = pl.program_id(1)

    @pl.when(i == 0)
    def _():
        xs_ref[0:HALO, :] = jnp.zeros((HALO, D_MODEL), BF16)

    @pl.when(i > 0)
    def _():
        xs_ref[0:HALO, :] = xs_ref[tm:tm + HALO, :]

    hm = h_ref[0] + _mm(ya_ref[0], wo_ref[0:wa, :]) + _mm(yb_ref[0], wo_ref[wa:, :])
    xs_ref[HALO:, :] = _rms(hm, g_ref[...]).astype(BF16)
    for c in range(D_FF // FF_CHUNK):
        cols = slice(c * FF_CHUNK, (c + 1) * FF_CHUNK)
        a_ref[...] = _mm(xs_ref[...], wup_ref[:, cols])
        gate = _mm(xs_ref[HALO:, :], wup_ref[:, D_FF + c * FF_CHUNK:D_FF + (c + 1) * FF_CHUNK])
        conv = (a_ref[pl.ds(HALO - 2, tm), :] * cw_ref[0:1, cols]
                + a_ref[pl.ds(HALO - 1, tm), :] * cw_ref[1:2, cols]
                + a_ref[pl.ds(HALO, tm), :] * cw_ref[2:3, cols]
                + cb_ref[:, cols])
        hg_ref[:, cols] = (_gelu(conv) * gate).astype(BF16)
    y = hm + _mm(hg_ref[...], wdown_ref[...])
    if final_norm:
        y = _rms(y, fg_ref[...])
    o_ref[0] = y


def _mix_ffn(h, ya, yb, col_a, col_b, w_out, gains, w_up, conv_w, conv_b, w_down, final_gain, layer, final_norm):
    bn, s, _ = h.shape
    tm = TOKEN_TILE
    wa, wb = ya.shape[2] // (col_a[1]), yb.shape[2] // (col_b[1])
    main = pl.BlockSpec((1, tm, D_MODEL), lambda b, i: (b, i, 0))
    return pl.pallas_call(
        functools.partial(_ffn_kernel, final_norm=final_norm),
        grid=(bn, s // tm),
        in_specs=[main,
                  pl.BlockSpec((1, tm, wa), lambda b, i: (b, i, col_a[0])),
                  pl.BlockSpec((1, tm, wb), lambda b, i: (b, i, col_b[0])),
                  _const_spec(w_out.shape), _layer_spec(gains.shape, layer), _layer_spec(w_up.shape, layer),
                  _layer_spec(conv_w.shape, layer), _layer_spec(conv_b.shape, layer),
                  _layer_spec(w_down.shape, layer), _const_spec((1, D_MODEL))],
        out_specs=main,
        out_shape=jax.ShapeDtypeStruct(h.shape, F32),
        scratch_shapes=[pltpu.VMEM((tm + HALO, D_MODEL), BF16), pltpu.VMEM((tm + HALO, FF_CHUNK), F32),
                        pltpu.VMEM((tm, D_FF), BF16)],
        compiler_params=_params(("parallel", "arbitrary")),
        name="mix_ffn",
    )(h, ya, yb, w_out, gains, w_up, conv_w, conv_b, w_down, final_gain)


def _rope(x, tab):
    c, s_up, s_lo = tab
    return x * c + pltpu.roll(x, ROPE_DIM // 2, 1) * s_up + pltpu.roll(x, LANES - ROPE_DIM // 2, 1) * s_lo


def _qkv_kernel(h_ref, pos_ref, g_ref, inv_ref, wqt_ref, wvt_ref, wn_ref,
                cqt_ref, ck_ref, cvt_ref, dq0_ref, dk0_ref, dv0_ref, dq1_ref, dk1_ref, dv1_ref,
                dq2_ref, dk2_ref, dv2_ref, xs_ref, tab_ref, perm_ref):
    tm = h_ref.shape[1]
    half = ROPE_DIM // 2
    xs_ref[...] = _rms(h_ref[0], g_ref[...]).astype(BF16)

    ang = inv_ref[...] * pos_ref[0].astype(F32)
    cos_t, sin_t = jnp.cos(ang), jnp.sin(ang)
    rest = HEAD_DIM - ROPE_DIM
    ones, z_half, z_rest = jnp.ones((rest, tm), F32), jnp.zeros((half, tm), F32), jnp.zeros((rest, tm), F32)
    tab_ref[0] = jnp.concatenate([cos_t, cos_t, ones] * 2, axis=0).T
    tab_ref[1] = jnp.concatenate([z_half, sin_t, z_rest] * 2, axis=0).T
    tab_ref[2] = jnp.concatenate([-sin_t, z_half, z_rest] * 2, axis=0).T

    tabs = (tab_ref[0], tab_ref[1], tab_ref[2])

    def project(col0, width, rope, scale):
        outs = []
        for c in range(width // MXU_COLS):
            r = _mm(xs_ref[...], wn_ref[:, col0 + c * MXU_COLS:col0 + (c + 1) * MXU_COLS])
            for x in (r[:, :LANES], r[:, LANES:]):
                if rope:
                    x = _rope(x, tabs)
                if scale is not None:
                    x = x * scale
                outs.append(x)
        return outs

    dscale = HEAD_DIM ** -0.5 * math.log2(math.e)
    gw = D_GROUP_WIDTH
    slot = 0
    group_refs = ((dq0_ref, dk0_ref, dv0_ref), (dq1_ref, dk1_ref, dv1_ref), (dq2_ref, dk2_ref, dv2_ref))
    for grp in reversed(range(len(D_PATTERNS))):
        dil = D_PATTERNS[grp][1]
        n = tm // dil
        for part in (2, 0, 1):
            ref, rope, scale = group_refs[grp][part], part < 2, dscale if part == 0 else None
            col0 = 3 * C_WIDTH + part * len(D_PATTERNS) * gw + grp * gw
            for c, res in enumerate(project(col0, gw, rope, scale)):
                lanes = slice(c * LANES, (c + 1) * LANES)
                if dil == 1:
                    ref[0, 0, :, lanes] = res.astype(BF16)
                else:
                    perm_ref[slot] = res
                    for r in range(dil):
                        ref[0, r, :, lanes] = perm_ref[slot, pl.ds(r, n, stride=dil), :].astype(BF16)
                    slot += 1

    for c, r in enumerate(project(C_WIDTH, C_WIDTH, True, None)):
        ck_ref[0, :, c * LANES:(c + 1) * LANES] = r.astype(BF16)

    qt = _mm_nt(wqt_ref[...], xs_ref[...])
    pieces = []
    for hd in range(C_WIDTH // HEAD_DIM):
        x1 = qt[hd * HEAD_DIM:hd * HEAD_DIM + half]
        x2 = qt[hd * HEAD_DIM + half:hd * HEAD_DIM + ROPE_DIM]
        pieces += [x1 * cos_t - x2 * sin_t, x2 * cos_t + x1 * sin_t, qt[hd * HEAD_DIM + ROPE_DIM:(hd + 1) * HEAD_DIM]]
    qt = jnp.concatenate(pieces, axis=0) * (HEAD_DIM ** -0.5 * math.log2(math.e))
    for hp in range(C_HEADS):
        cqt_ref[0, hp, 0] = qt[hp * LANES:(hp + 1) * LANES].astype(BF16)

    vt = _mm_nt(wvt_ref[...], xs_ref[...])
    for hp in range(C_HEADS):
        for j in range(tm // ATT_K):
            cvt_ref[0, hp, j] = vt[hp * LANES:(hp + 1) * LANES, j * ATT_K:(j + 1) * ATT_K].astype(BF16)


def _qkv_proj(h, positions, gain, inv_col, wq_t, wv_t, w_nat):
    bn, s, _ = h.shape
    tm = TOKEN_TILE
    main = pl.BlockSpec((1, tm, D_MODEL), lambda b, i: (b, i, 0))
    pos_spec = pl.BlockSpec((1, 1, tm), lambda b, i: (b, 0, i))
    out_shapes = [jax.ShapeDtypeStruct((bn, C_HEADS, s // ATT_Q, LANES, ATT_Q), BF16),
                  jax.ShapeDtypeStruct((bn, s, C_WIDTH), BF16),
                  jax.ShapeDtypeStruct((bn, C_HEADS, s // ATT_K, LANES, ATT_K), BF16)]
    out_specs = [pl.BlockSpec((1, C_HEADS, tm // ATT_Q, LANES, ATT_Q), lambda b, i: (b, 0, i, 0, 0)),
                 pl.BlockSpec((1, tm, C_WIDTH), lambda b, i: (b, i, 0)),
                 pl.BlockSpec((1, C_HEADS, tm // ATT_K, LANES, ATT_K), lambda b, i: (b, 0, i, 0, 0))]
    n_perm = 0
    for _, dil in D_PATTERNS:
        out_shapes += [jax.ShapeDtypeStruct((bn, dil, s // dil, D_GROUP_WIDTH), BF16)] * 3
        out_specs += [pl.BlockSpec((1, dil, tm // dil, D_GROUP_WIDTH), lambda b, i: (b, 0, i, 0))] * 3
        n_perm += 0 if dil == 1 else 3 * D_GROUP_WIDTH // LANES
    return pl.pallas_call(
        _qkv_kernel,
        grid=(bn, s // tm),
        in_specs=[main, pos_spec, _const_spec((1, D_MODEL)), _const_spec(inv_col.shape), _const_spec(wq_t.shape),
                  _const_spec(wv_t.shape), _const_spec(w_nat.shape)],
        out_specs=out_specs,
        out_shape=out_shapes,
        scratch_shapes=[pltpu.VMEM((tm, D_MODEL), BF16), pltpu.VMEM((3, tm, LANES), F32),
                        pltpu.VMEM((n_perm, tm, LANES), F32)],
        compiler_params=_params(("parallel", "parallel")),
        name="qkv_proj",
    )(h, positions.reshape(bn, 1, s), gain, inv_col, wq_t, wv_t, w_nat)


def _diff_kernel(qt_ref, k_ref, vt_ref, lam_ref, sg_ref, o_ref, qc_ref, s_ref, cm_ref, e_ref, m_ref, acc_ref):
    tq, tk = ATT_Q, ATT_K
    streams = range(qt_ref.shape[1])
    n_strips = 2 * tq // LANES
    all_strips = tuple(range(n_strips))
    late = tuple(j for j in all_strips if (j * LANES) % tq >= tk)
    ones = jnp.ones((V_ROWS - LANES, tk), BF16)
    row = lax.broadcasted_iota(jnp.int32, (LANES, tq), 0)
    lam_rows = lam_ref[...]
    lam = (jnp.exp(jnp.sum(lam_rows[0:1] * lam_rows[1:2], axis=-1, keepdims=True))
           - jnp.exp(jnp.sum(lam_rows[2:3] * lam_rows[3:4], axis=-1, keepdims=True)) + LAMBDA_INIT)
    out_gain = sg_ref[...] * (1.0 - LAMBDA_INIT)

    def qk(st, kb, buf, strips=all_strips):
        start = pl.multiple_of(kb * tk, tk)
        rhs = qc_ref[st] if strips == all_strips else jnp.concatenate(
            [qc_ref[st, :, j * LANES:(j + 1) * LANES] for j in strips], axis=1)
        s = _mm(k_ref[0, pl.ds(start, tk), st * LANES:(st + 1) * LANES], rhs)
        for idx, j in enumerate(strips):
            strip = s[:, idx * LANES:(idx + 1) * LANES]
            s_ref[st, buf, j, 0:tk, :] = strip
            cm_ref[st, buf, :, j * LANES:(j + 1) * LANES] = jnp.max(strip, axis=0, keepdims=True)

    def soft(st, buf, kb, strips=all_strips, key_off=None):
        alphas = []
        for j in strips:
            cols = slice(j * LANES, (j + 1) * LANES)
            s = s_ref[st, buf, j, 0:tk, :]
            q_rel = (j * LANES) % tq
            col_max = cm_ref[st, buf, :, cols]
            if key_off is not None and q_rel < key_off + tk:
                key = key_off + lax.broadcasted_iota(jnp.int32, (tk, LANES), 0)
                qry = q_rel + lax.broadcasted_iota(jnp.int32, (tk, LANES), 1)
                s = jnp.where(key <= qry, s, NEG_BIG)
                col_max = jnp.max(s, axis=0, keepdims=True)
            m_old = m_ref[st, :, cols]
            m_new = jnp.maximum(m_old, col_max)
            alphas.append(jnp.exp2(m_old - m_new))
            m_ref[st, :, cols] = m_new
            e_ref[st, j, 0:tk, :] = jnp.exp2(s - m_new).astype(BF16)
        lhs = jnp.concatenate([vt_ref[0, st, kb], ones], axis=0)
        pv = _mm(lhs, jnp.concatenate([e_ref[st, j, 0:tk, :] for j in strips], axis=1))
        if strips == all_strips:
            acc_ref[st] = acc_ref[st] * jnp.concatenate(alphas, axis=1) + pv
        else:
            for idx, j in enumerate(strips):
                cols = slice(j * LANES, (j + 1) * LANES)
                acc_ref[st, :, cols] = acc_ref[st, :, cols] * alphas[idx] + pv[:, idx * LANES:(idx + 1) * LANES]

    def load_queries(st, qj):
        qt = qt_ref[0, st, qj]
        zero = jnp.zeros_like(qt)
        qc_ref[st, :, :tq] = jnp.where(row < HEAD_DIM, qt, zero)
        qc_ref[st, :, tq:] = jnp.where(row >= HEAD_DIM, qt, zero)

    def finish(st, qj):
        inv = 1.0 / acc_ref[st, LANES:LANES + 1, :]
        ot = acc_ref[st, 0:LANES, :tq] * inv[:, :tq] - lam * (acc_ref[st, 0:LANES, tq:] * inv[:, tq:])
        ot = ot * lax.rsqrt(jnp.mean(ot * ot, axis=0, keepdims=True) + EPS) * out_gain
        o_ref[0, pl.ds(pl.multiple_of(qj * tq, tq), tq), st * LANES:(st + 1) * LANES] = ot.T.astype(BF16)

    n_blocks = qt_ref.shape[2]
    for st in streams:
        load_queries(st, 0)
        qk(st, 0, 0)

    def query_block(qj, _):
        m_ref[...] = jnp.full(m_ref.shape, NEG_BIG, F32)
        acc_ref[...] = jnp.zeros(acc_ref.shape, F32)

        def pair(it, _):
            ns = len(streams)
            for st in streams:
                qk(st, 2 * it + 1, 1)
            for st in range(ns + 1):
                if st < ns:
                    soft(st, 0, 2 * it)
                    qk(st, 2 * it + 2, 0)
                if st >= 1:
                    soft(st - 1, 1, 2 * it + 1)
            return 0

        lax.fori_loop(0, qj, pair, 0)
        for st in streams:
            qk(st, 2 * qj + 1, 1, late)
            soft(st, 0, 2 * qj, key_off=0)
            load_queries(st, jnp.minimum(qj + 1, n_blocks - 1))
            qk(st, 0, 0)
            soft(st, 1, 2 * qj + 1, late, key_off=tk)
            finish(st, qj)
        return 0

    lax.fori_loop(0, n_blocks, query_block, 0)


def _diff_attention(cqt, ck, cvt, lam_rows, subln_gain_col):
    bn, s, _ = ck.shape
    tq, tk, ns = ATT_Q, ATT_K, ATT_STREAMS
    return pl.pallas_call(
        _diff_kernel,
        grid=(bn, C_HEADS // ns),
        in_specs=[pl.BlockSpec((1, ns, s // tq, LANES, tq), lambda b, h: (b, h, 0, 0, 0)),
                  pl.BlockSpec((1, s, ns * LANES), lambda b, h: (b, 0, h)),
                  pl.BlockSpec((1, ns, s // tk, LANES, tk), lambda b, h: (b, h, 0, 0, 0)),
                  pl.BlockSpec(lam_rows.shape, lambda b, h: (0, 0)),
                  pl.BlockSpec((LANES, 1), lambda b, h: (0, 0))],
        out_specs=pl.BlockSpec((1, s, ns * LANES), lambda b, h: (b, 0, h)),
        out_shape=jax.ShapeDtypeStruct((bn, s, C_WIDTH), BF16),
        scratch_shapes=[pltpu.VMEM((ns, LANES, 2 * tq), BF16),
                        pltpu.VMEM((ns, 2, 2 * tq // LANES, tk + STRIP_PAD, LANES), F32),
                        pltpu.VMEM((ns, 2, 1, 2 * tq), F32),
                        pltpu.VMEM((ns, 2 * tq // LANES, tk + 2 * STRIP_PAD, LANES), BF16), pltpu.VMEM((ns, 1, 2 * tq), F32),
                        pltpu.VMEM((ns, V_ROWS, 2 * tq), F32)],
        compiler_params=_params(("parallel", "parallel")),
        name="diff_attention",
    )(cqt, ck, cvt, lam_rows, subln_gain_col)


def _dil_kernel(q0, k0, v0, q1, k1, v1, q2, k2, v2, o_ref, os_ref, ls_ref):
    nk = DIL_BLOCK
    s_len = o_ref.shape[1]
    lane = lax.broadcasted_iota(jnp.int32, (nk, LANES), 1)
    first_head = lane < HEAD_DIM
    row = lax.broadcasted_iota(jnp.int32, (2 * nk, 2 * nk), 0) % nk
    col = lax.broadcasted_iota(jnp.int32, (2 * nk, 2 * nk), 1)
    band = ((col < nk) & (col >= row)) | ((col >= nk) & (col - nk <= row))
    ones = jnp.ones((2 * nk, LANES), BF16)

    for grp, ((_, dil), q_ref, k_ref, v_ref) in enumerate(zip(D_PATTERNS, (q0, q1, q2), (k0, k1, k2), (v0, v1, v2))):
        nb = s_len // dil // nk
        per_res = min(nb, DIL_BATCH)
        n_res = DIL_BATCH // per_res
        chunks = nb // per_res

        def batch(t, _, dil=dil, per_res=per_res, n_res=n_res, chunks=chunks, grp=grp,
                  q_ref=q_ref, k_ref=k_ref, v_ref=v_ref):
            n0 = (t % chunks) * per_res
            base = pl.multiple_of(n0 * nk, nk)
            prev = pl.multiple_of(jnp.maximum(n0 - 1, 0) * nk, nk)
            work = []
            for rr in range(n_res):
                r = (t // chunks) * n_res + rr
                q_all = q_ref[0, r, pl.ds(base, per_res * nk), :]
                k_all = jnp.concatenate([k_ref[0, r, pl.ds(prev, nk), :], k_ref[0, r, pl.ds(base, per_res * nk), :]], axis=0)
                v_all = jnp.concatenate([v_ref[0, r, pl.ds(prev, nk), :], v_ref[0, r, pl.ds(base, per_res * nk), :]], axis=0)
                for i in range(per_res):
                    work.append((r, i, q_all, k_all, v_all))

            def scores(item):
                r, i, q_all, k_all, v_all = item
                q = q_all[i * nk:(i + 1) * nk]
                k = k_all[i * nk:(i + 2) * nk]
                mask = band
                if i == 0:
                    mask = band & (col >= jnp.where(n0 == 0, nk, 0))
                zero = jnp.zeros_like(q)
                q2 = jnp.concatenate([jnp.where(first_head, q, zero), jnp.where(first_head, zero, q)], axis=0)
                return jnp.where(mask, _mm_nt(q2, k), NEG_BIG)

            def finish(item, s):
                r, i, q_all, k_all, v_all = item
                v = jnp.concatenate([v_all[i * nk:(i + 2) * nk], ones], axis=1)
                m = jnp.max(s, axis=-1, keepdims=True)
                pv = _mm(jnp.exp2(s - m).astype(BF16), v)
                num = jnp.where(first_head, pv[:nk, :LANES], pv[nk:, :LANES])
                den = jnp.where(first_head, pv[:nk, LANES:], pv[nk:, LANES:])
                top = jnp.where(first_head, jnp.broadcast_to(m[:nk], (nk, LANES)), jnp.broadcast_to(m[nk:], (nk, LANES)))
                rows = pl.ds((n0 + i) * (nk * dil) + r, nk, stride=dil)
                os_ref[grp, rows, :] = num / den
                ls_ref[grp, rows, :] = top + jnp.log2(den)

            pending = scores(work[0])
            for idx, item in enumerate(work):
                following = scores(work[idx + 1]) if idx + 1 < len(work) else None
                finish(item, pending)
                pending = following
            return 0

        lax.fori_loop(0, (dil // n_res) * chunks, batch, 0)

    tile = 512
    for t in range(s_len // tile):
        rows = slice(t * tile, (t + 1) * tile)
        lse = [ls_ref[grp, rows, :] for grp in range(3)]
        top = jnp.maximum(jnp.maximum(lse[0], lse[1]), lse[2])
        wts = [jnp.exp2(x - top) for x in lse]
        num = wts[0] * os_ref[0, rows, :] + wts[1] * os_ref[1, rows, :] + wts[2] * os_ref[2, rows, :]
        o_ref[0, rows, :] = (num / (wts[0] + wts[1] + wts[2])).astype(BF16)


def _dilated_attention(dqkv):
    bn = dqkv[0].shape[0]
    s = dqkv[0].shape[2]
    specs = []
    for (_, dil) in D_PATTERNS:
        specs += [pl.BlockSpec((1, dil, s // dil, LANES), lambda b, hp: (b, 0, 0, hp))] * 3
    return pl.pallas_call(
        _dil_kernel,
        grid=(bn, D_GROUP_WIDTH // LANES),
        in_specs=specs,
        out_specs=pl.BlockSpec((1, s, LANES), lambda b, hp: (b, 0, hp)),
        out_shape=jax.ShapeDtypeStruct((bn, s, D_GROUP_WIDTH), BF16),
        scratch_shapes=[pltpu.VMEM((3, s, LANES), F32), pltpu.VMEM((3, s, LANES), F32)],
        compiler_params=_params(("parallel", "parallel")),
        name="dilated_attention",
    )(*dqkv)


def kernel(x, positions, norm_mix, norm_ffn, final_norm, even_w_in, gmlp_v_gain, gmlp_w_s, gmlp_b_s, pool_w,
           pool_scale, even_w_out, odd_w_in, lambda_q1, lambda_k1, lambda_q2, lambda_k2, subln_gain, odd_w_out,
           ffn_w_up, ffn_conv_w, ffn_conv_b, ffn_w_down):
    row = lambda a: a.reshape(1, -1)
    bf = lambda a: a.astype(BF16)
    depth = norm_ffn.shape[0]

    later = [ffn_w_up.reshape(depth * D_MODEL, 2 * D_FF), ffn_w_down.reshape(depth * D_FF, D_MODEL),
             odd_w_in[0], odd_w_out[0], even_w_out[0]]
    y, (w_up, w_down, w_in, w_out_odd, w_out_even) = _even_mixer(
        x, row(norm_mix[0]), bf(even_w_in[0]), row(gmlp_v_gain[0]), gmlp_w_s[0], gmlp_b_s[0].T,
        bf(pool_w[0]), row(pool_scale[0]), later)
    ffn_params = (norm_ffn.reshape(depth, 1, D_MODEL), w_up.reshape(depth, D_MODEL, 2 * D_FF),
                  ffn_conv_w.reshape(depth, 3, D_FF), ffn_conv_b.reshape(depth, 1, D_FF),
                  w_down.reshape(depth, D_FF, D_MODEL), row(final_norm))
    h = _mix_ffn(x, y, y, (0, 2), (1, 2), w_out_even, *ffn_params, layer=0, final_norm=False)

    inv = ROPE_THETA ** (-jnp.arange(0, ROPE_DIM, 2, dtype=F32) / ROPE_DIM)
    qkv = _qkv_proj(h, positions, row(norm_mix[1]), inv.reshape(-1, 1), w_in[:, :C_WIDTH].T,
                    w_in[:, 2 * C_WIDTH:3 * C_WIDTH].T, w_in)
    lam_rows = jnp.stack([lambda_q1[0], lambda_k1[0], lambda_q2[0], lambda_k2[0]])
    yc = _diff_attention(qkv[0], qkv[1], qkv[2], lam_rows, subln_gain[0].reshape(-1, 1))
    yd = _dilated_attention(qkv[3:])
    return _mix_ffn(h, yc, yd, (0, 1), (0, 1), w_out_odd, *ffn_params, layer=1, final_norm=True)
```
